```python
import math
import jax, jax.numpy as jnp
from jax import lax
import numpy as np

D_MODEL = 1024
BATCH = 8
SEQ = 4096
DEPTH = 1

DA_HEADS = 4
DA_HEAD_DIM = 64
DA_V_DIM = 2 * DA_HEAD_DIM
DA_WIDTH = DA_HEADS * DA_V_DIM
QK_WIDTH = DA_HEADS * 2 * DA_HEAD_DIM
FT_GROUPS = 4
FT_GROUP_DIM = 128
FT_WIDTH = FT_GROUPS * FT_GROUP_DIM
MIX_WIDTH = DA_WIDTH + FT_WIDTH
IN_PROJ_WIDTH = 2 * QK_WIDTH + DA_WIDTH + FT_WIDTH
D_FF = 2816
CONV_WIDTH = 3
Q_BLOCK = 128
EPS = 1e-6
SUBLN_EPS = 1e-5

kernel_name = "hybrid_diffattn_fnet_convffn_encoder"


def rmsnorm(x, g, eps=EPS):
    xf = x.astype(jnp.float32)
    y = xf * lax.rsqrt(jnp.mean(xf * xf, axis=-1, keepdims=True) + eps) * g.astype(jnp.float32)
    return y.astype(x.dtype)


def alibi_slopes(n_heads):
    return jnp.array([2.0 ** (-8.0 * (h + 1) / n_heads) for h in range(n_heads)], dtype=jnp.float32)


def diff_attention(q, k, v, lam, slopes):
    B, S = q.shape[0], q.shape[1]
    nb = S // Q_BLOCK
    scale = DA_HEAD_DIM ** -0.5
    qb = q.reshape(B, nb, Q_BLOCK, DA_HEADS, 2, DA_HEAD_DIM).transpose(1, 0, 2, 3, 4, 5)
    pos_k = jnp.arange(S, dtype=jnp.int32)

    def block(args):
        qi, i = args
        pos_q = i * Q_BLOCK + jnp.arange(Q_BLOCK, dtype=jnp.int32)
        dist = jnp.abs(pos_q[:, None] - pos_k[None, :]).astype(jnp.float32)
        s = jnp.einsum('bqhcd,bkhcd->bhcqk', qi, k,
                       preferred_element_type=jnp.float32) * scale
        s = s - slopes[None, :, None, None, None] * dist[None, None, None]
        p = jax.nn.softmax(s, axis=-1)
        a = p[:, :, 0] - lam * p[:, :, 1]
        return jnp.einsum('bhqk,bkhv->bqhv', a.astype(v.dtype), v)

    o = lax.map(block, (qb, jnp.arange(nb, dtype=jnp.int32)))
    return o.transpose(1, 0, 2, 3, 4).reshape(B, S, DA_HEADS, DA_V_DIM)


def fourier_mix(u, w_ft):
    B, S = u.shape[0], u.shape[1]
    ug = u.reshape(B, S, FT_GROUPS, FT_GROUP_DIM).astype(jnp.float32)
    f = jnp.fft.fft2(ug, axes=(1, 3), norm="ortho").real
    out = jnp.einsum('bsgc,gcd->bsgd', f.astype(u.dtype), w_ft)
    return out.reshape(B, S, FT_WIDTH)


def centred_dwconv(h, w_conv, b_conv):
    hp = jnp.pad(h, ((0, 0), (1, 1), (0, 0)))
    return (w_conv[0] * hp[:, :-2] + w_conv[1] * hp[:, 1:-1] + w_conv[2] * hp[:, 2:] + b_conv)


def setup_inputs(seed: int = 0) -> dict:
    key = jax.random.key(seed)
    ks = jax.random.split(key, 16)
    f32 = jnp.float32
    nrm = lambda k, shape, s: (jax.random.normal(k, shape, f32) * s)
    return {
        "x": jax.random.normal(ks[0], (BATCH, SEQ, D_MODEL), f32),
        "g_mix": 1.0 + nrm(ks[1], (DEPTH, D_MODEL), 0.02),
        "w_in": nrm(ks[2], (DEPTH, D_MODEL, IN_PROJ_WIDTH), D_MODEL ** -0.5),
        "lambda_q1": nrm(ks[3], (DEPTH, DA_HEAD_DIM), 0.1),
        "lambda_k1": nrm(ks[4], (DEPTH, DA_HEAD_DIM), 0.1),
        "lambda_q2": nrm(ks[5], (DEPTH, DA_HEAD_DIM), 0.1),
        "lambda_k2": nrm(ks[6], (DEPTH, DA_HEAD_DIM), 0.1),
        "g_subln": 1.0 + nrm(ks[7], (DEPTH, DA_V_DIM), 0.02),
        "w_ft": nrm(ks[8], (DEPTH, FT_GROUPS, FT_GROUP_DIM, FT_GROUP_DIM), FT_GROUP_DIM ** -0.5),
        "w_out": nrm(ks[9], (DEPTH, MIX_WIDTH, D_MODEL), MIX_WIDTH ** -0.5),
        "g_ffn": 1.0 + nrm(ks[10], (DEPTH, D_MODEL), 0.02),
        "w_up": nrm(ks[11], (DEPTH, D_MODEL, 2 * D_FF), D_MODEL ** -0.5),
        "w_conv": nrm(ks[12], (DEPTH, CONV_WIDTH, 2 * D_FF), CONV_WIDTH ** -0.5),
        "b_conv": nrm(ks[13], (DEPTH, 2 * D_FF), 0.01),
        "w_down": nrm(ks[14], (DEPTH, D_FF, D_MODEL), D_FF ** -0.5),
        "g_final": 1.0 + nrm(ks[15], (D_MODEL,), 0.02),
    }


def reference(x, g_mix, w_in, lambda_q1, lambda_k1, lambda_q2, lambda_k2, g_subln, w_ft,
              w_out, g_ffn, w_up, w_conv, b_conv, w_down, g_final):
    B, S = x.shape[0], x.shape[1]
    slopes = alibi_slopes(DA_HEADS)
    for l in range(DEPTH):
        lambda_init = 0.8 - 0.6 * math.exp(-0.3 * l)
        h = rmsnorm(x, g_mix[l])
        z = h @ w_in[l]
        q = z[..., :QK_WIDTH].reshape(B, S, DA_HEADS, 2, DA_HEAD_DIM)
        k = z[..., QK_WIDTH:2 * QK_WIDTH].reshape(B, S, DA_HEADS, 2, DA_HEAD_DIM)
        v = z[..., 2 * QK_WIDTH:2 * QK_WIDTH + DA_WIDTH].reshape(B, S, DA_HEADS, DA_V_DIM)
        u = z[..., 2 * QK_WIDTH + DA_WIDTH:]
        lam = (jnp.exp(jnp.sum(lambda_q1[l].astype(jnp.float32) * lambda_k1[l].astype(jnp.float32)))
               - jnp.exp(jnp.sum(lambda_q2[l].astype(jnp.float32) * lambda_k2[l].astype(jnp.float32)))
               + lambda_init)
        o_da = diff_attention(q, k, v, lam, slopes)
        o_da = rmsnorm(o_da, g_subln[l], SUBLN_EPS) * (1.0 - lambda_init)
        o_ft = fourier_mix(u, w_ft[l])
        mixed = jnp.concatenate([o_da.reshape(B, S, DA_WIDTH), o_ft], axis=-1)
        x = x + mixed @ w_out[l]
        h = rmsnorm(x, g_ffn[l])
        up = centred_dwconv(h @ w_up[l], w_conv[l], b_conv[l])
        gate, val = up[..., :D_FF], up[..., D_FF:]
        x = x + (jax.nn.silu(gate) * val) @ w_down[l]
    return rmsnorm(x, g_final)
```

```python
import functools
import math

import numpy as np
import jax
import jax.numpy as jnp
from jax.experimental import pallas as pl
from jax.experimental.pallas import tpu as pltpu

D_MODEL = 1024
BATCH = 8
SEQ = 4096
DA_HEADS = 4
DA_HEAD_DIM = 64
DA_V_DIM = 128
QK_WIDTH = 512
DA_WIDTH = 512
FT_GROUPS = 4
FT_GROUP_DIM = 128
FT_WIDTH = 512
IN_PROJ_WIDTH = 2048
D_FF = 2816
EPS = 1e-6
SUBLN_EPS = 1e-5
LAMBDA_INIT = 0.8 - 0.6 * math.exp(-0.3 * 0)
LOG2E = 1.4426950408889634

BF16 = jnp.bfloat16
F32 = jnp.float32

TM_IN = 512
TQ = 256
TK = TM_IN
V_ROWS = DA_V_DIM + 16
TM_DFT = 512
TM_FFN = 512
HALO = 16
FF_CHUNK = 256
VMEM_LIMIT = 56 * 1024 * 1024


def _dot(a, b):
    return jnp.dot(a, b, preferred_element_type=F32)


def _in_proj_kernel(x_ref, g_ref, w_ref, cs_ref, qt_ref, k_ref, vt_ref, xcs_ref):
    x = x_ref[...]
    ms = jnp.mean(x * x, axis=-1, keepdims=True)
    h = (x * jax.lax.rsqrt(ms + EPS) * g_ref[...]).astype(BF16)
    z = _dot(h, w_ref[...])
    row = jax.lax.broadcasted_iota(jnp.int32, (2 * DA_HEAD_DIM, TM_IN), 0)
    for hd in range(DA_HEADS):
        f = (DA_HEAD_DIM ** -0.5) / (2.0 ** (-8.0 * (hd + 1) / DA_HEADS))
        zq_t = (z[:, hd * 128:(hd + 1) * 128] * f).T
        qt_ref[0, hd, 0] = jnp.where(row < DA_HEAD_DIM, zq_t, 0.0).astype(BF16)
        qt_ref[0, hd, 1] = jnp.where(row >= DA_HEAD_DIM, zq_t, 0.0).astype(BF16)
        zv_t = z[:, 2 * QK_WIDTH + hd * 128:2 * QK_WIDTH + (hd + 1) * 128].T
        vt_ref[0, hd, 0, 0:DA_V_DIM, :] = zv_t.astype(BF16)
        ones_row = jax.lax.broadcasted_iota(jnp.int32, (V_ROWS - DA_V_DIM, TM_IN), 0) == 0
        vt_ref[0, hd, 0, DA_V_DIM:V_ROWS, :] = jnp.where(ones_row, 1.0, 0.0).astype(BF16)
    k_ref[...] = z[:, QK_WIDTH:2 * QK_WIDTH].astype(BF16)
    u0 = 2 * QK_WIDTH + DA_WIDTH
    for g in range(FT_GROUPS):
        ug = z[:, u0 + g * 128:u0 + (g + 1) * 128].astype(BF16)
        xcs = _dot(ug, cs_ref[...])
        xcs_ref[0, 0, :, g * 128:(g + 1) * 128] = xcs[:, :128].astype(BF16)
        xcs_ref[0, 1, :, g * 128:(g + 1) * 128] = xcs[:, 128:].astype(BF16)


def _in_proj(x2, g_mix, w_in, cs):
    nt = SEQ // TM_IN
    return pl.pallas_call(
        _in_proj_kernel,
        grid=(BATCH, nt),
        in_specs=[
            pl.BlockSpec((TM_IN, D_MODEL), lambda b, t: (b * nt + t, 0)),
            pl.BlockSpec((1, D_MODEL), lambda b, t: (0, 0)),
            pl.BlockSpec((D_MODEL, IN_PROJ_WIDTH), lambda b, t: (0, 0)),
            pl.BlockSpec((FT_GROUP_DIM, 2 * FT_GROUP_DIM), lambda b, t: (0, 0)),
        ],
        out_specs=[
            pl.BlockSpec((1, DA_HEADS, 2, 128, TM_IN), lambda b, t: (b, 0, 0, 0, t)),
            pl.BlockSpec((TM_IN, QK_WIDTH), lambda b, t: (b * nt + t, 0)),
            pl.BlockSpec((1, DA_HEADS, 1, V_ROWS, TM_IN), lambda b, t: (b, 0, t, 0, 0)),
            pl.BlockSpec((1, 2, TM_IN, FT_WIDTH), lambda b, t: (b, 0, t, 0)),
        ],
        out_shape=[
            jax.ShapeDtypeStruct((BATCH, DA_HEADS, 2, 128, SEQ), BF16),
            jax.ShapeDtypeStruct((BATCH * SEQ, QK_WIDTH), BF16),
            jax.ShapeDtypeStruct((BATCH, DA_HEADS, nt, V_ROWS, TM_IN), BF16),
            jax.ShapeDtypeStruct((BATCH, 2, SEQ, FT_WIDTH), BF16),
        ],
        compiler_params=pltpu.CompilerParams(
            dimension_semantics=("arbitrary", "arbitrary"), vmem_limit_bytes=VMEM_LIMIT),
        name="in_proj",
    )(x2, g_mix, w_in, cs)


def _attn_kernel(ch_ref, lq1_ref, lk1_ref, lq2_ref, lk2_ref, gs_ref, dist_ref,
                 qt_ref, k_ref, vt_ref, o_ref, acc_ref, m_ref):
    hd = pl.program_id(1)
    qi = pl.program_id(2)
    ch = ch_ref[hd]
    m_ref[...] = jnp.full(m_ref.shape, -1e30, F32)
    acc_ref[...] = jnp.zeros(acc_ref.shape, F32)

    def kv_step(t, carry):
        k_t = k_ref[pl.ds(pl.multiple_of(t * TK, TK), TK), :]
        r0 = pl.multiple_of(t * TK - qi * TQ + (SEQ - TQ), 128)
        nd = dist_ref[pl.ds(r0, TK), :]
        for c in range(2):
            s = _dot(k_t, qt_ref[0, 0, c]) + nd
            m_old = m_ref[c]
            m_new = jnp.maximum(m_old, jnp.max(s, axis=0, keepdims=True))
            alpha = jnp.exp2((m_old - m_new) * ch)
            p = jnp.exp2((s - m_new) * ch).astype(BF16)
            pv = _dot(vt_ref[0, 0, t], p)
            acc_ref[c] = alpha * acc_ref[c] + pv
            m_ref[c] = m_new
        return carry

    jax.lax.fori_loop(0, SEQ // TK, kv_step, 0)

    lam = (jnp.exp(jnp.sum(lq1_ref[...] * lk1_ref[...], axis=-1, keepdims=True))
           - jnp.exp(jnp.sum(lq2_ref[...] * lk2_ref[...], axis=-1, keepdims=True))
           + LAMBDA_INIT)
    a1 = acc_ref[0]
    a2 = acc_ref[1]
    o1 = a1[:DA_V_DIM] / a1[DA_V_DIM:DA_V_DIM + 1]
    o2 = a2[:DA_V_DIM] / a2[DA_V_DIM:DA_V_DIM + 1]
    o = o1 - lam * o2
    ms = jnp.mean(o * o, axis=0, keepdims=True)
    y = o * jax.lax.rsqrt(ms + SUBLN_EPS) * gs_ref[...] * (1.0 - LAMBDA_INIT)
    o_ref[...] = y.T.astype(BF16)


def _attention(ch, lq1, lk1, lq2, lk2, gs_col, ndist, qt, kz, vt):
    nq = SEQ // TQ
    nt = SEQ // TK
    vec = pl.BlockSpec((1, DA_HEAD_DIM), lambda b, h, u: (0, 0))
    return pl.pallas_call(
        _attn_kernel,
        grid=(BATCH, DA_HEADS, nq),
        in_specs=[
            pl.BlockSpec(memory_space=pltpu.SMEM),
            vec, vec, vec, vec,
            pl.BlockSpec((DA_V_DIM, 1), lambda b, h, u: (0, 0)),
            pl.BlockSpec((2 * SEQ - TQ, TQ), lambda b, h, u: (0, 0)),
            pl.BlockSpec((1, 1, 2, 128, TQ), lambda b, h, u: (b, h, 0, 0, u)),
            pl.BlockSpec((SEQ, 128), lambda b, h, u: (b, h)),
            pl.BlockSpec((1, 1, nt, V_ROWS, TK), lambda b, h, u: (b, h, 0, 0, 0)),
        ],
        out_specs=pl.BlockSpec((TQ, DA_V_DIM), lambda b, h, u: (b * nq + u, h)),
        out_shape=jax.ShapeDtypeStruct((BATCH * SEQ, DA_WIDTH), BF16),
        scratch_shapes=[pltpu.VMEM((2, V_ROWS, TQ), F32), pltpu.VMEM((2, 1, TQ), F32)],
        compiler_params=pltpu.CompilerParams(
            dimension_semantics=("arbitrary", "arbitrary", "arbitrary"),
            vmem_limit_bytes=VMEM_LIMIT),
        name="diff_attention",
    )(ch, lq1, lk1, lq2, lk2, gs_col, ndist, qt, kz, vt)


def _seq_dft_kernel(t1c_ref, t1s_ref, t2c_ref, t2s_ref, x_ref, wft_ref, o_ref, c_ref, s_ref):
    it = pl.program_id(0)
    b = pl.program_id(1)

    @pl.when(b == 0)
    def _build():
        for j1l in range(TM_DFT // 64):
            j1 = it * (TM_DFT // 64) + j1l
            c1 = t1c_ref[pl.ds(j1, 1), :]
            s1 = t1s_ref[pl.ds(j1, 1), :]
            c2 = t2c_ref[...]
            s2 = t2s_ref[...]
            c_ref[j1l * 64:(j1l + 1) * 64, :] = (c1 * c2 - s1 * s2).astype(BF16)
            s_ref[j1l * 64:(j1l + 1) * 64, :] = (-(s1 * c2 + c1 * s2)).astype(BF16)

    y = _dot(c_ref[...], x_ref[0, 0]) + _dot(s_ref[...], x_ref[0, 1])
    for g in range(FT_GROUPS):
        yg = y[:, g * 128:(g + 1) * 128].astype(BF16)
        o_ref[:, g * 128:(g + 1) * 128] = _dot(yg, wft_ref[g]).astype(BF16)


def _seq_dft(t1c, t1s, t2c, t2s, xcs, w_ft):
    ni = SEQ // TM_DFT
    tab = pl.BlockSpec((64, SEQ), lambda i, b: (0, 0))
    return pl.pallas_call(
        _seq_dft_kernel,
        grid=(ni, BATCH),
        in_specs=[
            tab, tab, tab, tab,
            pl.BlockSpec((1, 2, SEQ, FT_WIDTH), lambda i, b: (b, 0, 0, 0)),
            pl.BlockSpec((FT_GROUPS, FT_GROUP_DIM, FT_GROUP_DIM), lambda i, b: (0, 0, 0)),
        ],
        out_specs=pl.BlockSpec((TM_DFT, FT_WIDTH), lambda i, b: (b * ni + i, 0)),
        out_shape=jax.ShapeDtypeStruct((BATCH * SEQ, FT_WIDTH), BF16),
        scratch_shapes=[pltpu.VMEM((TM_DFT, SEQ), BF16), pltpu.VMEM((TM_DFT, SEQ), BF16)],
        compiler_params=pltpu.CompilerParams(
            dimension_semantics=("arbitrary", "arbitrary"), vmem_limit_bytes=VMEM_LIMIT),
        name="seq_dft",
    )(t1c, t1s, t2c, t2s, xcs, w_ft)


def _rms(x, g, eps):
    ms = jnp.mean(x * x, axis=-1, keepdims=True)
    return x * jax.lax.rsqrt(ms + eps) * g


def _ffn_kernel(xc_ref, xp_ref, xn_ref, dac_ref, dap_ref, dan_ref, ftc_ref, ftp_ref, ftn_ref,
                wo_ref, gf_ref, wup_ref, wcv_ref, bcv_ref, wdn_ref, gl_ref, o_ref, act_ref):
    i = pl.program_id(0)
    tiles_per_seq = SEQ // TM_FFN
    pos = i % tiles_per_seq
    xe = jnp.concatenate([xp_ref[...], xc_ref[...], xn_ref[...]], axis=0)
    da = jnp.concatenate([dap_ref[...], dac_ref[...], dan_ref[...]], axis=0)
    ft = jnp.concatenate([ftp_ref[...], ftc_ref[...], ftn_ref[...]], axis=0)
    x1 = xe + _dot(da, wo_ref[0:DA_WIDTH, :]) + _dot(ft, wo_ref[DA_WIDTH:, :])
    h2 = _rms(x1, gf_ref[...], EPS)
    r = jax.lax.broadcasted_iota(jnp.int32, (TM_FFN + 2 * HALO, 1), 0)
    valid = jnp.logical_and(jnp.logical_or(r >= HALO, pos > 0),
                            jnp.logical_or(r < TM_FFN + HALO, pos < tiles_per_seq - 1))
    h2 = jnp.where(valid, h2, 0.0).astype(BF16)
    n_ext = TM_FFN + 2 * HALO

    def conv(up, col0):
        w = wcv_ref[:, col0:col0 + FF_CHUNK]
        prev = pltpu.roll(up, 1, 0)[HALO:HALO + TM_FFN]
        nxt = pltpu.roll(up, n_ext - 1, 0)[HALO:HALO + TM_FFN]
        cur = up[HALO:HALO + TM_FFN]
        return (w[0:1] * prev + w[1:2] * cur + w[2:3] * nxt
                + bcv_ref[:, col0:col0 + FF_CHUNK])

    for j in range(D_FF // FF_CHUNK):
        c0 = j * FF_CHUNK
        gate = conv(_dot(h2, wup_ref[:, c0:c0 + FF_CHUNK]), c0)
        val = conv(_dot(h2, wup_ref[:, D_FF + c0:D_FF + c0 + FF_CHUNK]), D_FF + c0)
        act = gate / (1.0 + jnp.exp(-gate)) * val
        act_ref[:, c0:c0 + FF_CHUNK] = act.astype(BF16)

    y = x1[HALO:HALO + TM_FFN] + _dot(act_ref[...], wdn_ref[...])
    o_ref[...] = _rms(y, gl_ref[...], EPS)


def _ffn(x2, o_da, o_ft, w_out, g_ffn, w_up, w_conv, b_conv, w_down, g_final):
    n = BATCH * SEQ // TM_FFN
    hb = TM_FFN // HALO
    last = BATCH * SEQ // HALO - 1
    cur = lambda i: (i, 0)
    prv = lambda i: (jnp.maximum(i * hb - 1, 0), 0)
    nxt = lambda i: (jnp.minimum((i + 1) * hb, last), 0)
    const = lambda i: (0, 0)

    def trio(width):
        return [pl.BlockSpec((TM_FFN, width), cur), pl.BlockSpec((HALO, width), prv),
                pl.BlockSpec((HALO, width), nxt)]

    def resident(shape):
        return pl.BlockSpec(shape, const, pipeline_mode=pl.Buffered(1))

    return pl.pallas_call(
        _ffn_kernel,
        grid=(n,),
        in_specs=trio(D_MODEL) + trio(DA_WIDTH) + trio(FT_WIDTH) + [
            resident((D_MODEL, D_MODEL)),
            resident((1, D_MODEL)),
            resident((D_MODEL, 2 * D_FF)),
            resident((3, 2 * D_FF)),
            resident((1, 2 * D_FF)),
            resident((D_FF, D_MODEL)),
            resident((1, D_MODEL)),
        ],
        out_specs=pl.BlockSpec((TM_FFN, D_MODEL), cur),
        out_shape=jax.ShapeDtypeStruct((BATCH * SEQ, D_MODEL), F32),
        scratch_shapes=[pltpu.VMEM((TM_FFN, D_FF), BF16)],
        compiler_params=pltpu.CompilerParams(
            dimension_semantics=("arbitrary",), vmem_limit_bytes=VMEM_LIMIT),
        name="out_proj_ffn",
    )(x2, x2, x2, o_da, o_da, o_da, o_ft, o_ft, o_ft,
      w_out, g_ffn, w_up, w_conv, b_conv, w_down, g_final)


@functools.lru_cache(maxsize=None)
def _dft_tables():
    k = np.arange(SEQ, dtype=np.int64)[None, :]
    j = np.arange(64, dtype=np.int64)[:, None]
    a1 = 2.0 * np.pi * ((j * k) % 64).astype(np.float64) / 64.0
    a2 = 2.0 * np.pi * ((j * k) % SEQ).astype(np.float64) / SEQ
    norm = 1.0 / math.sqrt(SEQ)
    t1c, t1s = np.cos(a1), np.sin(a1)
    t2c, t2s = np.cos(a2) * norm, np.sin(a2) * norm
    c = np.arange(FT_GROUP_DIM, dtype=np.int64)
    ac = 2.0 * np.pi * ((c[:, None] * c[None, :]) % FT_GROUP_DIM) / FT_GROUP_DIM
    cn = 1.0 / math.sqrt(FT_GROUP_DIM)
    cs = np.concatenate([np.cos(ac) * cn, np.sin(ac) * cn], axis=1)
    f = lambda a: np.asarray(a, dtype=np.float32)
    return f(t1c), f(t1s), f(t2c), f(t2s), f(cs)


def kernel(x, g_mix, w_in, lambda_q1, lambda_k1, lambda_q2, lambda_k2, g_subln, w_ft, w_out,
           g_ffn, w_up, w_conv, b_conv, w_down, g_final):
    x2 = x.reshape(BATCH * SEQ, D_MODEL)
    t1c, t1s, t2c, t2s, cs = _dft_tables()
    qt, kz, vt, xcs = _in_proj(x2, g_mix[0][None, :], w_in[0].astype(BF16),
                               jnp.asarray(cs).astype(BF16))

    r = jnp.arange(2 * SEQ - TQ, dtype=jnp.int32)[:, None]
    c = jnp.arange(TQ, dtype=jnp.int32)[None, :]
    ndist = -jnp.abs(r - (SEQ - TQ) - c).astype(F32)
    ch = jnp.asarray([LOG2E * 2.0 ** (-8.0 * (h + 1) / DA_HEADS) for h in range(DA_HEADS)], F32)
    o_da = _attention(ch, lambda_q1, lambda_k1, lambda_q2, lambda_k2,
                      g_subln[0][:, None], ndist, qt, kz, vt)

    o_ft = _seq_dft(jnp.asarray(t1c), jnp.asarray(t1s), jnp.asarray(t2c), jnp.asarray(t2s),
                    xcs, w_ft[0].astype(BF16))

    y = _ffn(x2, o_da, o_ft, w_out[0].astype(BF16), g_ffn[0][None, :], w_up[0].astype(BF16),
             w_conv[0], b_conv[0][None, :], w_down[0].astype(BF16), g_final[None, :])
    return y.reshape(BATCH, SEQ, D_MODEL)
```

```python
import functools
import math

import numpy as np
import jax
import jax.numpy as jnp
from jax.experimental import pallas as pl
from jax.experimental.pallas import tpu as pltpu

D_MODEL = 1024
BATCH = 8
SEQ = 4096
DA_HEADS = 4
DA_HEAD_DIM = 64
DA_V_DIM = 128
QK_WIDTH = 512
DA_WIDTH = 512
FT_GROUPS = 4
FT_GROUP_DIM = 128
FT_WIDTH = 512
IN_PROJ_WIDTH = 2048
D_FF = 2816
EPS = 1e-6
SUBLN_EPS = 1e-5
LAMBDA_INIT = 0.8 - 0.6 * math.exp(-0.3 * 0)
LOG2E = 1.4426950408889634

BF16 = jnp.bfloat16
F32 = jnp.float32

TM_IN = 1024
TQ = 256
CH = TM_IN
V_ROWS = DA_V_DIM + 16
TM_DFT = 512
TM_FFN = 512
HALO = 16
FF_CHUNK = 256
VMEM_LIMIT = 56 * 1024 * 1024


def _dot(a, b):
    return jnp.dot(a, b, preferred_element_type=F32)


def _in_proj_kernel(x_ref, g_ref, w_ref, cs_ref, qt_ref, k_ref, vt_ref, xcs_ref):
    x = x_ref[...]
    ms = jnp.mean(x * x, axis=-1, keepdims=True)
    h = (x * jax.lax.rsqrt(ms + EPS) * g_ref[...]).astype(BF16)
    z = _dot(h, w_ref[...])
    row = jax.lax.broadcasted_iota(jnp.int32, (2 * DA_HEAD_DIM, TM_IN), 0)
    for hd in range(DA_HEADS):
        f = (DA_HEAD_DIM ** -0.5) / (2.0 ** (-8.0 * (hd + 1) / DA_HEADS))
        zq_t = (z[:, hd * 128:(hd + 1) * 128] * f).T
        q1 = jnp.where(row < DA_HEAD_DIM, zq_t, 0.0).astype(BF16)
        q2 = jnp.where(row >= DA_HEAD_DIM, zq_t, 0.0).astype(BF16)
        for j in range(TM_IN // TQ):
            qt_ref[0, hd, j, 0] = q1[:, j * TQ:(j + 1) * TQ]
            qt_ref[0, hd, j, 1] = q2[:, j * TQ:(j + 1) * TQ]
        zv_t = z[:, 2 * QK_WIDTH + hd * 128:2 * QK_WIDTH + (hd + 1) * 128].T
        vt_ref[0, hd, 0, 0:DA_V_DIM, :] = zv_t.astype(BF16)
        ones_row = jax.lax.broadcasted_iota(jnp.int32, (V_ROWS - DA_V_DIM, TM_IN), 0) == 0
        vt_ref[0, hd, 0, DA_V_DIM:V_ROWS, :] = jnp.where(ones_row, 1.0, 0.0).astype(BF16)
    k_ref[...] = z[:, QK_WIDTH:2 * QK_WIDTH].astype(BF16)
    u0 = 2 * QK_WIDTH + DA_WIDTH
    for g in range(FT_GROUPS):
        ug = z[:, u0 + g * 128:u0 + (g + 1) * 128].astype(BF16)
        xcs = _dot(ug, cs_ref[...])
        xcs_ref[0, 0, :, g * 128:(g + 1) * 128] = xcs[:, :128].astype(BF16)
        xcs_ref[0, 1, :, g * 128:(g + 1) * 128] = xcs[:, 128:].astype(BF16)


def _in_proj(x2, g_mix, w_in, cs):
    nt = SEQ // TM_IN
    return pl.pallas_call(
        _in_proj_kernel,
        grid=(BATCH, nt),
        in_specs=[
            pl.BlockSpec((TM_IN, D_MODEL), lambda b, t: (b * nt + t, 0)),
            pl.BlockSpec((1, D_MODEL), lambda b, t: (0, 0)),
            pl.BlockSpec((D_MODEL, IN_PROJ_WIDTH), lambda b, t: (0, 0)),
            pl.BlockSpec((FT_GROUP_DIM, 2 * FT_GROUP_DIM), lambda b, t: (0, 0)),
        ],
        out_specs=[
            pl.BlockSpec((1, DA_HEADS, TM_IN // TQ, 2, 128, TQ),
                         lambda b, t: (b, 0, t, 0, 0, 0)),
            pl.BlockSpec((TM_IN, QK_WIDTH), lambda b, t: (b * nt + t, 0)),
            pl.BlockSpec((1, DA_HEADS, 1, V_ROWS, TM_IN), lambda b, t: (b, 0, t, 0, 0)),
            pl.BlockSpec((1, 2, TM_IN, FT_WIDTH), lambda b, t: (b, 0, t, 0)),
        ],
        out_shape=[
            jax.ShapeDtypeStruct((BATCH, DA_HEADS, SEQ // TQ, 2, 128, TQ), BF16),
            jax.ShapeDtypeStruct((BATCH * SEQ, QK_WIDTH), BF16),
            jax.ShapeDtypeStruct((BATCH, DA_HEADS, nt, V_ROWS, TM_IN), BF16),
            jax.ShapeDtypeStruct((BATCH, 2, SEQ, FT_WIDTH), BF16),
        ],
        compiler_params=pltpu.CompilerParams(
            dimension_semantics=("arbitrary", "arbitrary"), vmem_limit_bytes=VMEM_LIMIT),
        name="in_proj",
    )(x2, g_mix, w_in, cs)


def _attn_kernel(ch_ref, lq1_ref, lk1_ref, lq2_ref, lk2_ref, gs_ref, dist_ref,
                 qt_ref, k_ref, vt_ref, o_ref, sa_ref, sb_ref, ma_ref, mb_ref, acc_ref):
    hd = pl.program_id(1)
    ch = ch_ref[hd]
    bufs = ((sa_ref, ma_ref), (sb_ref, mb_ref))

    def rows_of(r):
        return pl.ds(pl.multiple_of(r * CH, CH), CH)

    def reset(slot):
        bufs[slot][1][...] = jnp.full((2, 1, TQ), -1e30, F32)

    def scores_chunk(u, r, slot):
        s_ref, m_ref = bufs[slot]
        k_r = k_ref[rows_of(r), :]
        r0 = pl.multiple_of(r * CH - u * TQ + (SEQ - TQ), 128)
        nd = dist_ref[pl.ds(r0, CH), :]
        for c in range(2):
            s = _dot(k_r, qt_ref[0, 0, u, c]) + nd
            s_ref[c, rows_of(r), :] = s
            m_ref[c] = jnp.maximum(m_ref[c], jnp.max(s, axis=0, keepdims=True))

    def softmax_chunk(r, slot):
        s_ref, m_ref = bufs[slot]
        for c in range(2):
            p = jnp.exp2((s_ref[c, rows_of(r), :] - m_ref[c]) * ch).astype(BF16)
            acc_ref[c] += _dot(vt_ref[0, 0, r], p)

    def finalize(u):
        lam = (jnp.exp(jnp.sum(lq1_ref[...] * lk1_ref[...], axis=-1, keepdims=True))
               - jnp.exp(jnp.sum(lq2_ref[...] * lk2_ref[...], axis=-1, keepdims=True))
               + LAMBDA_INIT)
        a1 = acc_ref[0]
        a2 = acc_ref[1]
        o1 = a1[:DA_V_DIM] / a1[DA_V_DIM:DA_V_DIM + 1]
        o2 = a2[:DA_V_DIM] / a2[DA_V_DIM:DA_V_DIM + 1]
        o = o1 - lam * o2
        ms = jnp.mean(o * o, axis=0, keepdims=True)
        y = o * jax.lax.rsqrt(ms + SUBLN_EPS) * gs_ref[...] * (1.0 - LAMBDA_INIT)
        o_ref[pl.ds(pl.multiple_of(u * TQ, TQ), TQ), :] = y.T.astype(BF16)
        acc_ref[...] = jnp.zeros(acc_ref.shape, F32)

    nch = SEQ // CH
    nq = SEQ // TQ
    acc_ref[...] = jnp.zeros(acc_ref.shape, F32)
    reset(0)

    def first(r, carry):
        scores_chunk(0, r, 0)
        return carry

    jax.lax.fori_loop(0, nch, first, 0)

    def overlapped(u, slot):
        reset(slot)

        def chunk(r, carry):
            scores_chunk(u, r, slot)
            softmax_chunk(r, 1 - slot)
            return carry

        jax.lax.fori_loop(0, nch, chunk, 0, unroll=2)
        finalize(u - 1)

    def tile_pair(j, carry):
        overlapped(2 * j + 1, 1)
        overlapped(2 * j + 2, 0)
        return carry

    jax.lax.fori_loop(0, nq // 2 - 1, tile_pair, 0)
    overlapped(nq - 1, 1)

    def last(r, carry):
        softmax_chunk(r, 1)
        return carry

    jax.lax.fori_loop(0, nch, last, 0)
    finalize(nq - 1)


def _attention(ch, lq1, lk1, lq2, lk2, gs_col, ndist, qt, kz, vt):
    nq = SEQ // TQ
    nch = SEQ // CH
    vec = pl.BlockSpec((1, DA_HEAD_DIM), lambda b, h: (0, 0))
    return pl.pallas_call(
        _attn_kernel,
        grid=(BATCH, DA_HEADS),
        in_specs=[
            pl.BlockSpec(memory_space=pltpu.SMEM),
            vec, vec, vec, vec,
            pl.BlockSpec((DA_V_DIM, 1), lambda b, h: (0, 0)),
            pl.BlockSpec((2 * SEQ - TQ, TQ), lambda b, h: (0, 0), pipeline_mode=pl.Buffered(1)),
            pl.BlockSpec((1, 1, nq, 2, 128, TQ), lambda b, h: (b, h, 0, 0, 0, 0)),
            pl.BlockSpec((SEQ, 128), lambda b, h: (b, h)),
            pl.BlockSpec((1, 1, nch, V_ROWS, CH), lambda b, h: (b, h, 0, 0, 0)),
        ],
        out_specs=pl.BlockSpec((SEQ, DA_V_DIM), lambda b, h: (b, h)),
        out_shape=jax.ShapeDtypeStruct((BATCH * SEQ, DA_WIDTH), BF16),
        scratch_shapes=[pltpu.VMEM((2, SEQ, TQ), F32), pltpu.VMEM((2, SEQ, TQ), F32),
                        pltpu.VMEM((2, 1, TQ), F32), pltpu.VMEM((2, 1, TQ), F32),
                        pltpu.VMEM((2, V_ROWS, TQ), F32)],
        compiler_params=pltpu.CompilerParams(
            dimension_semantics=("arbitrary", "arbitrary"), vmem_limit_bytes=VMEM_LIMIT),
        name="diff_attention",
    )(ch, lq1, lk1, lq2, lk2, gs_col, ndist, qt, kz, vt)


def _seq_dft_kernel(t1c_ref, t1s_ref, t2c_ref, t2s_ref, x_ref, wft_ref, o_ref, c_ref, s_ref):
    it = pl.program_id(0)
    b = pl.program_id(1)

    @pl.when(b == 0)
    def _build():
        for j1l in range(TM_DFT // 64):
            j1 = it * (TM_DFT // 64) + j1l
            c1 = t1c_ref[pl.ds(j1, 1), :]
            s1 = t1s_ref[pl.ds(j1, 1), :]
            c2 = t2c_ref[...]
            s2 = t2s_ref[...]
            c_ref[j1l * 64:(j1l + 1) * 64, :] = (c1 * c2 - s1 * s2).astype(BF16)
            s_ref[j1l * 64:(j1l + 1) * 64, :] = (-(s1 * c2 + c1 * s2)).astype(BF16)

    y = _dot(c_ref[...], x_ref[0, 0]) + _dot(s_ref[...], x_ref[0, 1])
    for g in range(FT_GROUPS):
        yg = y[:, g * 128:(g + 1) * 128].astype(BF16)
        o_ref[:, g * 128:(g + 1) * 128] = _dot(yg, wft_ref[g]).astype(BF16)


def _seq_dft(t1c, t1s, t2c, t2s, xcs, w_ft):
    ni = SEQ // TM_DFT
    tab = pl.BlockSpec((64, SEQ), lambda i, b: (0, 0))
    return pl.pallas_call(
        _seq_dft_kernel,
        grid=(ni, BATCH),
        in_specs=[
            tab, tab, tab, tab,
            pl.BlockSpec((1, 2, SEQ, FT_WIDTH), lambda i, b: (b, 0, 0, 0)),
            pl.BlockSpec((FT_GROUPS, FT_GROUP_DIM, FT_GROUP_DIM), lambda i, b: (0, 0, 0)),
        ],
        out_specs=pl.BlockSpec((TM_DFT, FT_WIDTH), lambda i, b: (b * ni + i, 0)),
        out_shape=jax.ShapeDtypeStruct((BATCH * SEQ, FT_WIDTH), BF16),
        scratch_shapes=[pltpu.VMEM((TM_DFT, SEQ), BF16), pltpu.VMEM((TM_DFT, SEQ), BF16)],
        compiler_params=pltpu.CompilerParams(
            dimension_semantics=("arbitrary", "arbitrary"), vmem_limit_bytes=VMEM_LIMIT),
        name="seq_dft",
    )(t1c, t1s, t2c, t2s, xcs, w_ft)


def _rms(x, g, eps):
    ms = jnp.mean(x * x, axis=-1, keepdims=True)
    return x * jax.lax.rsqrt(ms + eps) * g


def _ffn_kernel(xc_ref, xp_ref, xn_ref, dac_ref, dap_ref, dan_ref, ftc_ref, ftp_ref, ftn_ref,
                wo_ref, gf_ref, wup_ref, wcv_ref, bcv_ref, wdn_ref, gl_ref, o_ref, act_ref):
    i = pl.program_id(0)
    tiles_per_seq = SEQ // TM_FFN
    pos = i % tiles_per_seq
    xe = jnp.concatenate([xp_ref[...], xc_ref[...], xn_ref[...]], axis=0)
    da = jnp.concatenate([dap_ref[...], dac_ref[...], dan_ref[...]], axis=0)
    ft = jnp.concatenate([ftp_ref[...], ftc_ref[...], ftn_ref[...]], axis=0)
    x1 = xe + _dot(da, wo_ref[0:DA_WIDTH, :]) + _dot(ft, wo_ref[DA_WIDTH:, :])
    h2 = _rms(x1, gf_ref[...], EPS)
    r = jax.lax.broadcasted_iota(jnp.int32, (TM_FFN + 2 * HALO, 1), 0)
    valid = jnp.logical_and(jnp.logical_or(r >= HALO, pos > 0),
                            jnp.logical_or(r < TM_FFN + HALO, pos < tiles_per_seq - 1))
    h2 = jnp.where(valid, h2, 0.0).astype(BF16)
    n_ext = TM_FFN + 2 * HALO

    def conv(up, col0):
        w = wcv_ref[:, col0:col0 + FF_CHUNK]
        prev = pltpu.roll(up, 1, 0)[HALO:HALO + TM_FFN]
        nxt = pltpu.roll(up, n_ext - 1, 0)[HALO:HALO + TM_FFN]
        cur = up[HALO:HALO + TM_FFN]
        return (w[0:1] * prev + w[1:2] * cur + w[2:3] * nxt
                + bcv_ref[:, col0:col0 + FF_CHUNK])

    for j in range(D_FF // FF_CHUNK):
        c0 = j * FF_CHUNK
        gate = conv(_dot(h2, wup_ref[:, c0:c0 + FF_CHUNK]), c0)
        val = conv(_dot(h2, wup_ref[:, D_FF + c0:D_FF + c0 + FF_CHUNK]), D_FF + c0)
        act = gate / (1.0 + jnp.exp(-gate)) * val
        act_ref[:, c0:c0 + FF_CHUNK] = act.astype(BF16)

    y = x1[HALO:HALO + TM_FFN] + _dot(act_ref[...], wdn_ref[...])
    o_ref[...] = _rms(y, gl_ref[...], EPS)


def _ffn(x2, o_da, o_ft, w_out, g_ffn, w_up, w_conv, b_conv, w_down, g_final):
    n = BATCH * SEQ // TM_FFN
    hb = TM_FFN // HALO
    last = BATCH * SEQ // HALO - 1
    cur = lambda i: (i, 0)
    prv = lambda i: (jnp.maximum(i * hb - 1, 0), 0)
    nxt = lambda i: (jnp.minimum((i + 1) * hb, last), 0)
    const = lambda i: (0, 0)

    def trio(width):
        return [pl.BlockSpec((TM_FFN, width), cur), pl.BlockSpec((HALO, width), prv),
                pl.BlockSpec((HALO, width), nxt)]

    def resident(shape):
        return pl.BlockSpec(shape, const, pipeline_mode=pl.Buffered(1))

    return pl.pallas_call(
        _ffn_kernel,
        grid=(n,),
        in_specs=trio(D_MODEL) + trio(DA_WIDTH) + trio(FT_WIDTH) + [
            resident((D_MODEL, D_MODEL)),
            resident((1, D_MODEL)),
            resident((D_MODEL, 2 * D_FF)),
            resident((3, 2 * D_FF)),
            resident((1, 2 * D_FF)),
            resident((D_FF, D_MODEL)),
            resident((1, D_MODEL)),
        ],
        out_specs=pl.BlockSpec((TM_FFN, D_MODEL), cur),
        out_shape=jax.ShapeDtypeStruct((BATCH * SEQ, D_MODEL), F32),
        scratch_shapes=[pltpu.VMEM((TM_FFN, D_FF), BF16)],
        compiler_params=pltpu.CompilerParams(
            dimension_semantics=("arbitrary",), vmem_limit_bytes=VMEM_LIMIT),
        name="out_proj_ffn",
    )(x2, x2, x2, o_da, o_da, o_da, o_ft, o_ft, o_ft,
      w_out, g_ffn, w_up, w_conv, b_conv, w_down, g_final)


@functools.lru_cache(maxsize=None)
def _dft_tables():
    k = np.arange(SEQ, dtype=np.int64)[None, :]
    j = np.arange(64, dtype=np.int64)[:, None]
    a1 = 2.0 * np.pi * ((j * k) % 64).astype(np.float64) / 64.0
    a2 = 2.0 * np.pi * ((j * k) % SEQ).astype(np.float64) / SEQ
    norm = 1.0 / math.sqrt(SEQ)
    t1c, t1s = np.cos(a1), np.sin(a1)
    t2c, t2s = np.cos(a2) * norm, np.sin(a2) * norm
    c = np.arange(FT_GROUP_DIM, dtype=np.int64)
    ac = 2.0 * np.pi * ((c[:, None] * c[None, :]) % FT_GROUP_DIM) / FT_GROUP_DIM
    cn = 1.0 / math.sqrt(FT_GROUP_DIM)
    cs = np.concatenate([np.cos(ac) * cn, np.sin(ac) * cn], axis=1)
    f = lambda a: np.asarray(a, dtype=np.float32)
    return f(t1c), f(t1s), f(t2c), f(t2s), f(cs)


def kernel(x, g_mix, w_in, lambda_q1, lambda_k1, lambda_q2, lambda_k2, g_subln, w_ft, w_out,
           g_ffn, w_up, w_conv, b_conv, w_down, g_final):
    x2 = x.reshape(BATCH * SEQ, D_MODEL)
    t1c, t1s, t2c, t2s, cs = _dft_tables()
    qt, kz, vt, xcs = _in_proj(x2, g_mix[0][None, :], w_in[0].astype(BF16),
                               jnp.asarray(cs).astype(BF16))

    r = jnp.arange(2 * SEQ - TQ, dtype=jnp.int32)[:, None]
    c = jnp.arange(TQ, dtype=jnp.int32)[None, :]
    ndist = -jnp.abs(r - (SEQ - TQ) - c).astype(F32)
    ch = jnp.asarray([LOG2E * 2.0 ** (-8.0 * (h + 1) / DA_HEADS) for h in range(DA_HEADS)], F32)
    o_da = _attention(ch, lambda_q1, lambda_k1, lambda_q2, lambda_k2,
                      g_subln[0][:, None], ndist, qt, kz, vt)

    o_ft = _seq_dft(jnp.asarray(t1c), jnp.asarray(t1s), jnp.asarray(t2c), jnp.asarray(t2s),
                    xcs, w_ft[0].astype(BF16))

    y = _ffn(x2, o_da, o_ft, w_out[0].astype(BF16), g_ffn[0][None, :], w_up[0].astype(BF16),
             w_conv[0], b_conv[0][None, :], w_down[0].astype(BF16), g_final[None, :])
    return y.reshape(BATCH, SEQ, D_MODEL)
```

```python
import functools
import math

import numpy as np
import jax
import jax.numpy as jnp
from jax.experimental import pallas as pl
from jax.experimental.pallas import tpu as pltpu

D_MODEL = 1024
BATCH = 8
SEQ = 4096
DA_HEADS = 4
DA_HEAD_DIM = 64
DA_V_DIM = 128
QK_WIDTH = 512
DA_WIDTH = 512
FT_GROUPS = 4
FT_GROUP_DIM = 128
FT_WIDTH = 512
IN_PROJ_WIDTH = 2048
D_FF = 2816
EPS = 1e-6
SUBLN_EPS = 1e-5
LAMBDA_INIT = 0.8 - 0.6 * math.exp(-0.3 * 0)
LOG2E = 1.4426950408889634

BF16 = jnp.bfloat16
F32 = jnp.float32

TM_IN = 1024
TQ = 256
CH = TM_IN
V_ROWS = DA_V_DIM + 16
TM_DFT = 512
TM_FFN = 512
HALO = 16
FF_CHUNK = 256
VMEM_LIMIT = 56 * 1024 * 1024


def _dot(a, b):
    return jnp.dot(a, b, preferred_element_type=F32)


def _in_proj_kernel(x_ref, g_ref, w_ref, cs_ref, qt_ref, k_ref, vt_ref, xcs_ref):
    x = x_ref[...]
    ms = jnp.mean(x * x, axis=-1, keepdims=True)
    h = (x * jax.lax.rsqrt(ms + EPS) * g_ref[...]).astype(BF16)
    z = _dot(h, w_ref[...])
    row = jax.lax.broadcasted_iota(jnp.int32, (2 * DA_HEAD_DIM, TM_IN), 0)
    for hd in range(DA_HEADS):
        zq_t = (z[:, hd * 128:(hd + 1) * 128] * (DA_HEAD_DIM ** -0.5 * LOG2E)).T
        q1 = jnp.where(row < DA_HEAD_DIM, zq_t, 0.0).astype(BF16)
        q2 = jnp.where(row >= DA_HEAD_DIM, zq_t, 0.0).astype(BF16)
        for j in range(TM_IN // TQ):
            qt_ref[0, hd, j, 0] = q1[:, j * TQ:(j + 1) * TQ]
            qt_ref[0, hd, j, 1] = q2[:, j * TQ:(j + 1) * TQ]
        zv_t = z[:, 2 * QK_WIDTH + hd * 128:2 * QK_WIDTH + (hd + 1) * 128].T
        vt_ref[0, hd, 0, 0:DA_V_DIM, :] = zv_t.astype(BF16)
        ones_row = jax.lax.broadcasted_iota(jnp.int32, (V_ROWS - DA_V_DIM, TM_IN), 0) == 0
        vt_ref[0, hd, 0, DA_V_DIM:V_ROWS, :] = jnp.where(ones_row, 1.0, 0.0).astype(BF16)
    k_ref[...] = z[:, QK_WIDTH:2 * QK_WIDTH].astype(BF16)
    u0 = 2 * QK_WIDTH + DA_WIDTH
    for g in range(FT_GROUPS):
        ug = z[:, u0 + g * 128:u0 + (g + 1) * 128].astype(BF16)
        xcs = _dot(ug, cs_ref[...])
        xcs_ref[0, 0, :, g * 128:(g + 1) * 128] = xcs[:, :128].astype(BF16)
        xcs_ref[0, 1, :, g * 128:(g + 1) * 128] = xcs[:, 128:].astype(BF16)


def _in_proj(x2, g_mix, w_in, cs):
    nt = SEQ // TM_IN
    return pl.pallas_call(
        _in_proj_kernel,
        grid=(BATCH, nt),
        in_specs=[
            pl.BlockSpec((TM_IN, D_MODEL), lambda b, t: (b * nt + t, 0)),
            pl.BlockSpec((1, D_MODEL), lambda b, t: (0, 0)),
            pl.BlockSpec((D_MODEL, IN_PROJ_WIDTH), lambda b, t: (0, 0)),
            pl.BlockSpec((FT_GROUP_DIM, 2 * FT_GROUP_DIM), lambda b, t: (0, 0)),
        ],
        out_specs=[
            pl.BlockSpec((1, DA_HEADS, TM_IN // TQ, 2, 128, TQ),
                         lambda b, t: (b, 0, t, 0, 0, 0)),
            pl.BlockSpec((TM_IN, QK_WIDTH), lambda b, t: (b * nt + t, 0)),
            pl.BlockSpec((1, DA_HEADS, 1, V_ROWS, TM_IN), lambda b, t: (b, 0, t, 0, 0)),
            pl.BlockSpec((1, 2, TM_IN, FT_WIDTH), lambda b, t: (b, 0, t, 0)),
        ],
        out_shape=[
            jax.ShapeDtypeStruct((BATCH, DA_HEADS, SEQ // TQ, 2, 128, TQ), BF16),
            jax.ShapeDtypeStruct((BATCH * SEQ, QK_WIDTH), BF16),
            jax.ShapeDtypeStruct((BATCH, DA_HEADS, nt, V_ROWS, TM_IN), BF16),
            jax.ShapeDtypeStruct((BATCH, 2, SEQ, FT_WIDTH), BF16),
        ],
        compiler_params=pltpu.CompilerParams(
            dimension_semantics=("arbitrary", "arbitrary"), vmem_limit_bytes=VMEM_LIMIT),
        name="in_proj",
    )(x2, g_mix, w_in, cs)


def _attn_kernel(ch_ref, lq1_ref, lk1_ref, lq2_ref, lk2_ref, gs_ref,
                 qt_ref, k_ref, vt_ref, o_ref, bias_ref, sa_ref, sb_ref, ma_ref, mb_ref,
                 acc_ref):
    hd = pl.program_id(0)
    bufs = ((sa_ref, ma_ref), (sb_ref, mb_ref))

    @pl.when(pl.program_id(1) == 0)
    def _build_bias():
        nslope = -ch_ref[hd]
        d = (jax.lax.broadcasted_iota(jnp.int32, (TQ, TQ), 0)
             - jax.lax.broadcasted_iota(jnp.int32, (TQ, TQ), 1))

        def rows(i, carry):
            off = i * TQ - (SEQ - TQ)
            bias_ref[pl.ds(pl.multiple_of(i * TQ, TQ), TQ), :] = (
                jnp.abs(d + off).astype(F32) * nslope)
            return carry

        jax.lax.fori_loop(0, (2 * SEQ - TQ) // TQ, rows, 0)

    def rows_of(r):
        return pl.ds(pl.multiple_of(r * CH, CH), CH)

    def reset(slot):
        bufs[slot][1][...] = jnp.full((2, 1, TQ), -1e30, F32)

    def scores_chunk(u, r, slot):
        s_ref, m_ref = bufs[slot]
        k_r = k_ref[rows_of(r), :]
        r0 = pl.multiple_of(r * CH - u * TQ + (SEQ - TQ), 128)
        bias = bias_ref[pl.ds(r0, CH), :]
        for c in range(2):
            s = _dot(k_r, qt_ref[0, 0, u, c]) + bias
            s_ref[c, rows_of(r), :] = s
            m_ref[c] = jnp.maximum(m_ref[c], jnp.max(s, axis=0, keepdims=True))

    def softmax_chunk(r, slot):
        s_ref, m_ref = bufs[slot]
        for c in range(2):
            p = jnp.exp2(s_ref[c, rows_of(r), :] - m_ref[c]).astype(BF16)
            acc_ref[c] += _dot(vt_ref[0, 0, r], p)

    def finalize(u):
        lam = (jnp.exp(jnp.sum(lq1_ref[...] * lk1_ref[...], axis=-1, keepdims=True))
               - jnp.exp(jnp.sum(lq2_ref[...] * lk2_ref[...], axis=-1, keepdims=True))
               + LAMBDA_INIT)
        a1 = acc_ref[0]
        a2 = acc_ref[1]
        o1 = a1[:DA_V_DIM] / a1[DA_V_DIM:DA_V_DIM + 1]
        o2 = a2[:DA_V_DIM] / a2[DA_V_DIM:DA_V_DIM + 1]
        o = o1 - lam * o2
        ms = jnp.mean(o * o, axis=0, keepdims=True)
        y = o * jax.lax.rsqrt(ms + SUBLN_EPS) * gs_ref[...] * (1.0 - LAMBDA_INIT)
        o_ref[pl.ds(pl.multiple_of(u * TQ, TQ), TQ), :] = y.T.astype(BF16)
        acc_ref[...] = jnp.zeros(acc_ref.shape, F32)

    nch = SEQ // CH
    nq = SEQ // TQ
    acc_ref[...] = jnp.zeros(acc_ref.shape, F32)
    reset(0)

    def first(r, carry):
        scores_chunk(0, r, 0)
        return carry

    jax.lax.fori_loop(0, nch, first, 0)

    def overlapped(u, slot):
        reset(slot)

        def chunk(r, carry):
            scores_chunk(u, r, slot)
            softmax_chunk(r, 1 - slot)
            return carry

        jax.lax.fori_loop(0, nch, chunk, 0, unroll=True)
        finalize(u - 1)

    def tile_pair(j, carry):
        overlapped(2 * j + 1, 1)
        overlapped(2 * j + 2, 0)
        return carry

    jax.lax.fori_loop(0, nq // 2 - 1, tile_pair, 0)
    overlapped(nq - 1, 1)

    def last(r, carry):
        softmax_chunk(r, 1)
        return carry

    jax.lax.fori_loop(0, nch, last, 0)
    finalize(nq - 1)


def _attention(ch, lq1, lk1, lq2, lk2, gs_col, qt, kz, vt):
    nq = SEQ // TQ
    nch = SEQ // CH
    vec = pl.BlockSpec((1, DA_HEAD_DIM), lambda h, b: (0, 0))
    return pl.pallas_call(
        _attn_kernel,
        grid=(DA_HEADS, BATCH),
        in_specs=[
            pl.BlockSpec(memory_space=pltpu.SMEM),
            vec, vec, vec, vec,
            pl.BlockSpec((DA_V_DIM, 1), lambda h, b: (0, 0)),
            pl.BlockSpec((1, 1, nq, 2, 128, TQ), lambda h, b: (b, h, 0, 0, 0, 0)),
            pl.BlockSpec((SEQ, 128), lambda h, b: (b, h)),
            pl.BlockSpec((1, 1, nch, V_ROWS, CH), lambda h, b: (b, h, 0, 0, 0)),
        ],
        out_specs=pl.BlockSpec((SEQ, DA_V_DIM), lambda h, b: (b, h)),
        out_shape=jax.ShapeDtypeStruct((BATCH * SEQ, DA_WIDTH), BF16),
        scratch_shapes=[pltpu.VMEM((2 * SEQ - TQ, TQ), F32),
                        pltpu.VMEM((2, SEQ, TQ), F32), pltpu.VMEM((2, SEQ, TQ), F32),
                        pltpu.VMEM((2, 1, TQ), F32), pltpu.VMEM((2, 1, TQ), F32),
                        pltpu.VMEM((2, V_ROWS, TQ), F32)],
        compiler_params=pltpu.CompilerParams(
            dimension_semantics=("arbitrary", "arbitrary"), vmem_limit_bytes=VMEM_LIMIT),
        name="diff_attention",
    )(ch, lq1, lk1, lq2, lk2, gs_col, qt, kz, vt)


def _seq_dft_kernel(t1c_ref, t1s_ref, t2c_ref, t2s_ref, x_ref, wft_ref, o_ref, c_ref, s_ref):
    it = pl.program_id(0)
    b = pl.program_id(1)

    @pl.when(b == 0)
    def _build():
        for j1l in range(TM_DFT // 64):
            j1 = it * (TM_DFT // 64) + j1l
            c1 = t1c_ref[pl.ds(j1, 1), :]
            s1 = t1s_ref[pl.ds(j1, 1), :]
            c2 = t2c_ref[...]
            s2 = t2s_ref[...]
            c_ref[j1l * 64:(j1l + 1) * 64, :] = (c1 * c2 - s1 * s2).astype(BF16)
            s_ref[j1l * 64:(j1l + 1) * 64, :] = (-(s1 * c2 + c1 * s2)).astype(BF16)

    y = _dot(c_ref[...], x_ref[0, 0]) + _dot(s_ref[...], x_ref[0, 1])
    for g in range(FT_GROUPS):
        yg = y[:, g * 128:(g + 1) * 128].astype(BF16)
        o_ref[:, g * 128:(g + 1) * 128] = _dot(yg, wft_ref[g]).astype(BF16)


def _seq_dft(t1c, t1s, t2c, t2s, xcs, w_ft):
    ni = SEQ // TM_DFT
    tab = pl.BlockSpec((64, SEQ), lambda i, b: (0, 0))
    return pl.pallas_call(
        _seq_dft_kernel,
        grid=(ni, BATCH),
        in_specs=[
            tab, tab, tab, tab,
            pl.BlockSpec((1, 2, SEQ, FT_WIDTH), lambda i, b: (b, 0, 0, 0)),
            pl.BlockSpec((FT_GROUPS, FT_GROUP_DIM, FT_GROUP_DIM), lambda i, b: (0, 0, 0)),
        ],
        out_specs=pl.BlockSpec((TM_DFT, FT_WIDTH), lambda i, b: (b * ni + i, 0)),
        out_shape=jax.ShapeDtypeStruct((BATCH * SEQ, FT_WIDTH), BF16),
        scratch_shapes=[pltpu.VMEM((TM_DFT, SEQ), BF16), pltpu.VMEM((TM_DFT, SEQ), BF16)],
        compiler_params=pltpu.CompilerParams(
            dimension_semantics=("arbitrary", "arbitrary"), vmem_limit_bytes=VMEM_LIMIT),
        name="seq_dft",
    )(t1c, t1s, t2c, t2s, xcs, w_ft)


def _rms(x, g, eps):
    ms = jnp.mean(x * x, axis=-1, keepdims=True)
    return x * jax.lax.rsqrt(ms + eps) * g


def _ffn_kernel(xc_ref, xp_ref, xn_ref, dac_ref, dap_ref, dan_ref, ftc_ref, ftp_ref, ftn_ref,
                wo_ref, gf_ref, wup_ref, wcv_ref, bcv_ref, wdn_ref, gl_ref, o_ref, act_ref):
    i = pl.program_id(0)
    tiles_per_seq = SEQ // TM_FFN
    pos = i % tiles_per_seq
    xe = jnp.concatenate([xp_ref[...], xc_ref[...], xn_ref[...]], axis=0)
    da = jnp.concatenate([dap_ref[...], dac_ref[...], dan_ref[...]], axis=0)
    ft = jnp.concatenate([ftp_ref[...], ftc_ref[...], ftn_ref[...]], axis=0)
    x1 = xe + _dot(da, wo_ref[0:DA_WIDTH, :]) + _dot(ft, wo_ref[DA_WIDTH:, :])
    h2 = _rms(x1, gf_ref[...], EPS)
    r = jax.lax.broadcasted_iota(jnp.int32, (TM_FFN + 2 * HALO, 1), 0)
    valid = jnp.logical_and(jnp.logical_or(r >= HALO, pos > 0),
                            jnp.logical_or(r < TM_FFN + HALO, pos < tiles_per_seq - 1))
    h2 = jnp.where(valid, h2, 0.0).astype(BF16)
    n_ext = TM_FFN + 2 * HALO

    def conv(up, col0):
        w = wcv_ref[:, col0:col0 + FF_CHUNK]
        prev = pltpu.roll(up, 1, 0)[HALO:HALO + TM_FFN]
        nxt = pltpu.roll(up, n_ext - 1, 0)[HALO:HALO + TM_FFN]
        cur = up[HALO:HALO + TM_FFN]
        return (w[0:1] * prev + w[1:2] * cur + w[2:3] * nxt
                + bcv_ref[:, col0:col0 + FF_CHUNK])

    for j in range(D_FF // FF_CHUNK):
        c0 = j * FF_CHUNK
        gate = conv(_dot(h2, wup_ref[:, c0:c0 + FF_CHUNK]), c0)
        val = conv(_dot(h2, wup_ref[:, D_FF + c0:D_FF + c0 + FF_CHUNK]), D_FF + c0)
        act = gate / (1.0 + jnp.exp(-gate)) * val
        act_ref[:, c0:c0 + FF_CHUNK] = act.astype(BF16)

    y = x1[HALO:HALO + TM_FFN] + _dot(act_ref[...], wdn_ref[...])
    o_ref[...] = _rms(y, gl_ref[...], EPS)


def _ffn(x2, o_da, o_ft, w_out, g_ffn, w_up, w_conv, b_conv, w_down, g_final):
    n = BATCH * SEQ // TM_FFN
    hb = TM_FFN // HALO
    last = BATCH * SEQ // HALO - 1
    cur = lambda i: (i, 0)
    prv = lambda i: (jnp.maximum(i * hb - 1, 0), 0)
    nxt = lambda i: (jnp.minimum((i + 1) * hb, last), 0)
    const = lambda i: (0, 0)

    def trio(width):
        return [pl.BlockSpec((TM_FFN, width), cur), pl.BlockSpec((HALO, width), prv),
                pl.BlockSpec((HALO, width), nxt)]

    def resident(shape):
        return pl.BlockSpec(shape, const, pipeline_mode=pl.Buffered(1))

    return pl.pallas_call(
        _ffn_kernel,
        grid=(n,),
        in_specs=trio(D_MODEL) + trio(DA_WIDTH) + trio(FT_WIDTH) + [
            resident((D_MODEL, D_MODEL)),
            resident((1, D_MODEL)),
            resident((D_MODEL, 2 * D_FF)),
            resident((3, 2 * D_FF)),
            resident((1, 2 * D_FF)),
            resident((D_FF, D_MODEL)),
            resident((1, D_MODEL)),
        ],
        out_specs=pl.BlockSpec((TM_FFN, D_MODEL), cur),
        out_shape=jax.ShapeDtypeStruct((BATCH * SEQ, D_MODEL), F32),
        scratch_shapes=[pltpu.VMEM((TM_FFN, D_FF), BF16)],
        compiler_params=pltpu.CompilerParams(
            dimension_semantics=("arbitrary",), vmem_limit_bytes=VMEM_LIMIT),
        name="out_proj_ffn",
    )(x2, x2, x2, o_da, o_da, o_da, o_ft, o_ft, o_ft,
      w_out, g_ffn, w_up, w_conv, b_conv, w_down, g_final)


@functools.lru_cache(maxsize=None)
def _dft_tables():
    k = np.arange(SEQ, dtype=np.int64)[None, :]
    j = np.arange(64, dtype=np.int64)[:, None]
    a1 = 2.0 * np.pi * ((j * k) % 64).astype(np.float64) / 64.0
    a2 = 2.0 * np.pi * ((j * k) % SEQ).astype(np.float64) / SEQ
    norm = 1.0 / math.sqrt(SEQ)
    t1c, t1s = np.cos(a1), np.sin(a1)
    t2c, t2s = np.cos(a2) * norm, np.sin(a2) * norm
    c = np.arange(FT_GROUP_DIM, dtype=np.int64)
    ac = 2.0 * np.pi * ((c[:, None] * c[None, :]) % FT_GROUP_DIM) / FT_GROUP_DIM
    cn = 1.0 / math.sqrt(FT_GROUP_DIM)
    cs = np.concatenate([np.cos(ac) * cn, np.sin(ac) * cn], axis=1)
    f = lambda a: np.asarray(a, dtype=np.float32)
    return f(t1c), f(t1s), f(t2c), f(t2s), f(cs)


def kernel(x, g_mix, w_in, lambda_q1, lambda_k1, lambda_q2, lambda_k2, g_subln, w_ft, w_out,
           g_ffn, w_up, w_conv, b_conv, w_down, g_final):
    x2 = x.reshape(BATCH * SEQ, D_MODEL)
    t1c, t1s, t2c, t2s, cs = _dft_tables()
    qt, kz, vt, xcs = _in_proj(x2, g_mix[0][None, :], w_in[0].astype(BF16),
                               jnp.asarray(cs).astype(BF16))

    ch = jnp.asarray([LOG2E * 2.0 ** (-8.0 * (h + 1) / DA_HEADS) for h in range(DA_HEADS)], F32)
    o_da = _attention(ch, lambda_q1, lambda_k1, lambda_q2, lambda_k2,
                      g_subln[0][:, None], qt, kz, vt)

    o_ft = _seq_dft(jnp.asarray(t1c), jnp.asarray(t1s), jnp.asarray(t2c), jnp.asarray(t2s),
                    xcs, w_ft[0].astype(BF16))

    y = _ffn(x2, o_da, o_ft, w_out[0].astype(BF16), g_ffn[0][None, :], w_up[0].astype(BF16),
             w_conv[0], b_conv[0][None, :], w_down[0].astype(BF16), g_final[None, :])
    return y.reshape(BATCH, SEQ, D_MODEL)
```

```python
import functools
import math

import numpy as np
import jax
import jax.numpy as jnp
from jax.experimental import pallas as pl
from jax.experimental.pallas import tpu as pltpu

D_MODEL = 1024
BATCH = 8
SEQ = 4096
DA_HEADS = 4
DA_HEAD_DIM = 64
DA_V_DIM = 128
QK_WIDTH = 512
DA_WIDTH = 512
FT_GROUPS = 4
FT_GROUP_DIM = 128
FT_WIDTH = 512
IN_PROJ_WIDTH = 2048
D_FF = 2816
EPS = 1e-6
SUBLN_EPS = 1e-5
LAMBDA_INIT = 0.8 - 0.6 * math.exp(-0.3 * 0)
LOG2E = 1.4426950408889634

BF16 = jnp.bfloat16
F32 = jnp.float32

TM_IN = 1024
TQ = 256
CH = TM_IN
V_ROWS = DA_V_DIM + 16
DFT_COLS = 256
P1 = 72
P2 = 136
TM_FFN = 512
HALO = 16
FF_CHUNK = 256
VMEM_LIMIT = 56 * 1024 * 1024


def _dot(a, b):
    return jnp.dot(a, b, preferred_element_type=F32)


def _in_proj_kernel(x_ref, g_ref, w_ref, cs_ref, wft_ref, qt_ref, k_ref, vt_ref, xcs_ref):
    x = x_ref[...]
    ms = jnp.mean(x * x, axis=-1, keepdims=True)
    h = (x * jax.lax.rsqrt(ms + EPS) * g_ref[...]).astype(BF16)
    z = _dot(h, w_ref[...])
    row = jax.lax.broadcasted_iota(jnp.int32, (2 * DA_HEAD_DIM, TM_IN), 0)
    for hd in range(DA_HEADS):
        zq_t = (z[:, hd * 128:(hd + 1) * 128] * (DA_HEAD_DIM ** -0.5 * LOG2E)).T
        q1 = jnp.where(row < DA_HEAD_DIM, zq_t, 0.0).astype(BF16)
        q2 = jnp.where(row >= DA_HEAD_DIM, zq_t, 0.0).astype(BF16)
        for j in range(TM_IN // TQ):
            qt_ref[0, hd, j, 0] = q1[:, j * TQ:(j + 1) * TQ]
            qt_ref[0, hd, j, 1] = q2[:, j * TQ:(j + 1) * TQ]
        zv_t = z[:, 2 * QK_WIDTH + hd * 128:2 * QK_WIDTH + (hd + 1) * 128].T
        vt_ref[0, hd, 0, 0:DA_V_DIM, :] = zv_t.astype(BF16)
        ones_row = jax.lax.broadcasted_iota(jnp.int32, (V_ROWS - DA_V_DIM, TM_IN), 0) == 0
        vt_ref[0, hd, 0, DA_V_DIM:V_ROWS, :] = jnp.where(ones_row, 1.0, 0.0).astype(BF16)
    k_ref[...] = z[:, QK_WIDTH:2 * QK_WIDTH].astype(BF16)
    u0 = 2 * QK_WIDTH + DA_WIDTH
    for g in range(FT_GROUPS):
        ug = z[:, u0 + g * 128:u0 + (g + 1) * 128].astype(BF16)
        wg = wft_ref[g]
        csw = jnp.concatenate([_dot(cs_ref[:, :128], wg), _dot(cs_ref[:, 128:], wg)],
                              axis=1).astype(BF16)
        xcs = _dot(ug, csw)
        xcs_ref[0, 0, :, g * 128:(g + 1) * 128] = xcs[:, :128].astype(BF16)
        xcs_ref[0, 1, :, g * 128:(g + 1) * 128] = xcs[:, 128:].astype(BF16)


def _in_proj(x2, g_mix, w_in, cs, w_ft):
    nt = SEQ // TM_IN
    return pl.pallas_call(
        _in_proj_kernel,
        grid=(BATCH, nt),
        in_specs=[
            pl.BlockSpec((TM_IN, D_MODEL), lambda b, t: (b * nt + t, 0)),
            pl.BlockSpec((1, D_MODEL), lambda b, t: (0, 0)),
            pl.BlockSpec((D_MODEL, IN_PROJ_WIDTH), lambda b, t: (0, 0)),
            pl.BlockSpec((FT_GROUP_DIM, 2 * FT_GROUP_DIM), lambda b, t: (0, 0)),
            pl.BlockSpec((FT_GROUPS, FT_GROUP_DIM, FT_GROUP_DIM), lambda b, t: (0, 0, 0)),
        ],
        out_specs=[
            pl.BlockSpec((1, DA_HEADS, TM_IN // TQ, 2, 128, TQ),
                         lambda b, t: (b, 0, t, 0, 0, 0)),
            pl.BlockSpec((TM_IN, QK_WIDTH), lambda b, t: (b * nt + t, 0)),
            pl.BlockSpec((1, DA_HEADS, 1, V_ROWS, TM_IN), lambda b, t: (b, 0, t, 0, 0)),
            pl.BlockSpec((1, 2, TM_IN, FT_WIDTH), lambda b, t: (b, 0, t, 0)),
        ],
        out_shape=[
            jax.ShapeDtypeStruct((BATCH, DA_HEADS, SEQ // TQ, 2, 128, TQ), BF16),
            jax.ShapeDtypeStruct((BATCH * SEQ, QK_WIDTH), BF16),
            jax.ShapeDtypeStruct((BATCH, DA_HEADS, nt, V_ROWS, TM_IN), BF16),
            jax.ShapeDtypeStruct((BATCH, 2, SEQ, FT_WIDTH), BF16),
        ],
        compiler_params=pltpu.CompilerParams(
            dimension_semantics=("arbitrary", "arbitrary"), vmem_limit_bytes=VMEM_LIMIT),
        name="in_proj",
    )(x2, g_mix, w_in, cs, w_ft)


def _attn_kernel(ch_ref, lq1_ref, lk1_ref, lq2_ref, lk2_ref, gs_ref,
                 qt_ref, k_ref, vt_ref, o_ref, bias_ref, sa_ref, sb_ref, ma_ref, mb_ref,
                 acc_ref):
    hd = pl.program_id(0)
    bufs = ((sa_ref, ma_ref), (sb_ref, mb_ref))

    @pl.when(pl.program_id(1) == 0)
    def _build_bias():
        nslope = -ch_ref[hd]
        d = (jax.lax.broadcasted_iota(jnp.int32, (TQ, TQ), 0)
             - jax.lax.broadcasted_iota(jnp.int32, (TQ, TQ), 1))

        def rows(i, carry):
            off = i * TQ - (SEQ - TQ)
            bias_ref[pl.ds(pl.multiple_of(i * TQ, TQ), TQ), :] = (
                jnp.abs(d + off).astype(F32) * nslope)
            return carry

        jax.lax.fori_loop(0, (2 * SEQ - TQ) // TQ, rows, 0)

    def rows_of(r):
        return pl.ds(pl.multiple_of(r * CH, CH), CH)

    def reset(slot):
        bufs[slot][1][...] = jnp.full((2, 1, TQ), -1e30, F32)

    def scores_chunk(u, r, slot):
        s_ref, m_ref = bufs[slot]
        k_r = k_ref[rows_of(r), :]
        r0 = pl.multiple_of(r * CH - u * TQ + (SEQ - TQ), 128)
        bias = bias_ref[pl.ds(r0, CH), :]
        for c in range(2):
            s = _dot(k_r, qt_ref[0, 0, u, c]) + bias
            s_ref[c, rows_of(r), :] = s
            m_ref[c] = jnp.maximum(m_ref[c], jnp.max(s, axis=0, keepdims=True))

    def softmax_chunk(r, slot):
        s_ref, m_ref = bufs[slot]
        for c in range(2):
            p = jnp.exp2(s_ref[c, rows_of(r), :] - m_ref[c]).astype(BF16)
            acc_ref[c] += _dot(vt_ref[0, 0, r], p)

    def finalize(u):
        lam = (jnp.exp(jnp.sum(lq1_ref[...] * lk1_ref[...], axis=-1, keepdims=True))
               - jnp.exp(jnp.sum(lq2_ref[...] * lk2_ref[...], axis=-1, keepdims=True))
               + LAMBDA_INIT)
        a1 = acc_ref[0]
        a2 = acc_ref[1]
        o1 = a1[:DA_V_DIM] / a1[DA_V_DIM:DA_V_DIM + 1]
        o2 = a2[:DA_V_DIM] / a2[DA_V_DIM:DA_V_DIM + 1]
        o = o1 - lam * o2
        ms = jnp.mean(o * o, axis=0, keepdims=True)
        y = o * jax.lax.rsqrt(ms + SUBLN_EPS) * gs_ref[...] * (1.0 - LAMBDA_INIT)
        o_ref[pl.ds(pl.multiple_of(u * TQ, TQ), TQ), :] = y.T.astype(BF16)
        acc_ref[...] = jnp.zeros(acc_ref.shape, F32)

    nch = SEQ // CH
    nq = SEQ // TQ
    acc_ref[...] = jnp.zeros(acc_ref.shape, F32)
    reset(0)

    def first(r, carry):
        scores_chunk(0, r, 0)
        return carry

    jax.lax.fori_loop(0, nch, first, 0)

    def overlapped(u, slot):
        reset(slot)

        def chunk(r, carry):
            scores_chunk(u, r, slot)
            softmax_chunk(r, 1 - slot)
            return carry

        jax.lax.fori_loop(0, nch, chunk, 0, unroll=True)
        finalize(u - 1)

    def tile_pair(j, carry):
        overlapped(2 * j + 1, 1)
        overlapped(2 * j + 2, 0)
        return carry

    jax.lax.fori_loop(0, nq // 2 - 1, tile_pair, 0)
    overlapped(nq - 1, 1)

    def last(r, carry):
        softmax_chunk(r, 1)
        return carry

    jax.lax.fori_loop(0, nch, last, 0)
    finalize(nq - 1)


def _attention(ch, lq1, lk1, lq2, lk2, gs_col, qt, kz, vt):
    nq = SEQ // TQ
    nch = SEQ // CH
    vec = pl.BlockSpec((1, DA_HEAD_DIM), lambda h, b: (0, 0))
    return pl.pallas_call(
        _attn_kernel,
        grid=(DA_HEADS, BATCH),
        in_specs=[
            pl.BlockSpec(memory_space=pltpu.SMEM),
            vec, vec, vec, vec,
            pl.BlockSpec((DA_V_DIM, 1), lambda h, b: (0, 0)),
            pl.BlockSpec((1, 1, nq, 2, 128, TQ), lambda h, b: (b, h, 0, 0, 0, 0)),
            pl.BlockSpec((SEQ, 128), lambda h, b: (b, h)),
            pl.BlockSpec((1, 1, nch, V_ROWS, CH), lambda h, b: (b, h, 0, 0, 0)),
        ],
        out_specs=pl.BlockSpec((SEQ, DA_V_DIM), lambda h, b: (b, h)),
        out_shape=jax.ShapeDtypeStruct((BATCH * SEQ, DA_WIDTH), BF16),
        scratch_shapes=[pltpu.VMEM((2 * SEQ - TQ, TQ), F32),
                        pltpu.VMEM((2, SEQ, TQ), F32), pltpu.VMEM((2, SEQ, TQ), F32),
                        pltpu.VMEM((2, 1, TQ), F32), pltpu.VMEM((2, 1, TQ), F32),
                        pltpu.VMEM((2, V_ROWS, TQ), F32)],
        compiler_params=pltpu.CompilerParams(
            dimension_semantics=("arbitrary", "arbitrary"), vmem_limit_bytes=VMEM_LIMIT),
        name="diff_attention",
    )(ch, lq1, lk1, lq2, lk2, gs_col, qt, kz, vt)


def _seq_dft_kernel(x_ref, w1_ref, m2_ref, o_ref, xs_ref, a_ref, y_ref):
    nl = DFT_COLS // 128

    def slab(v, l):
        return v[:, l * 128:(l + 1) * 128]

    for n1 in range(64):
        for p in range(2):
            v = x_ref[0, p, n1 * 64:(n1 + 1) * 64, :].astype(F32)
            for l in range(nl):
                xs_ref[p * nl + l, n1 * P1:n1 * P1 + 64, :] = slab(v, l)

    def stage1(n2, carry):
        parts = [jnp.concatenate([xs_ref[p * nl + l, pl.ds(n2, 64, stride=P1), :]
                                  for l in range(nl)], axis=1) for p in range(2)]
        d = jnp.concatenate(parts, axis=0).astype(BF16)
        a = _dot(w1_ref[...], d)
        row0 = pl.multiple_of(n2 * P2, 8)
        for l in range(nl):
            a_ref[l, pl.ds(row0, 128), :] = slab(a, l)
        return carry

    jax.lax.fori_loop(0, 64, stage1, 0, unroll=8)

    def stage2(k1, carry):
        parts = [jnp.concatenate([a_ref[l, pl.ds(k1 + 64 * p, 64, stride=P2), :]
                                  for l in range(nl)], axis=1) for p in range(2)]
        d = jnp.concatenate(parts, axis=0).astype(BF16)
        y = _dot(m2_ref[k1], d)
        row0 = pl.multiple_of(k1 * P1, 8)
        for l in range(nl):
            y_ref[l, pl.ds(row0, 64), :] = slab(y, l)
        return carry

    jax.lax.fori_loop(0, 64, stage2, 0, unroll=8)

    def stage3(k2, carry):
        row0 = pl.multiple_of(k2 * 64, 64)
        for l in range(nl):
            o_ref[pl.ds(row0, 64), l * 128:(l + 1) * 128] = (
                y_ref[l, pl.ds(k2, 64, stride=P1), :].astype(BF16))
        return carry

    jax.lax.fori_loop(0, 64, stage3, 0, unroll=8)


def _seq_dft(xcs, w1, m2):
    nh = FT_WIDTH // DFT_COLS
    nl = DFT_COLS // 128
    return pl.pallas_call(
        _seq_dft_kernel,
        grid=(BATCH, nh),
        in_specs=[
            pl.BlockSpec((1, 2, SEQ, DFT_COLS), lambda b, j: (b, 0, 0, j)),
            pl.BlockSpec((128, 128), lambda b, j: (0, 0)),
            pl.BlockSpec((64, 64, 128), lambda b, j: (0, 0, 0)),
        ],
        out_specs=pl.BlockSpec((SEQ, DFT_COLS), lambda b, j: (b, j)),
        out_shape=jax.ShapeDtypeStruct((BATCH * SEQ, FT_WIDTH), BF16),
        scratch_shapes=[pltpu.VMEM((2 * nl, 64 * P1, 128), F32),
                        pltpu.VMEM((nl, 64 * P2, 128), F32),
                        pltpu.VMEM((nl, 64 * P1, 128), F32)],
        compiler_params=pltpu.CompilerParams(
            dimension_semantics=("arbitrary", "arbitrary"), vmem_limit_bytes=VMEM_LIMIT),
        name="seq_dft",
    )(xcs, w1, m2)


def _rms(x, g, eps):
    ms = jnp.mean(x * x, axis=-1, keepdims=True)
    return x * jax.lax.rsqrt(ms + eps) * g


def _ffn_kernel(xc_ref, xp_ref, xn_ref, dac_ref, dap_ref, dan_ref, ftc_ref, ftp_ref, ftn_ref,
                wo_ref, gf_ref, wup_ref, wcv_ref, bcv_ref, wdn_ref, gl_ref, o_ref, act_ref):
    i = pl.program_id(0)
    tiles_per_seq = SEQ // TM_FFN
    pos = i % tiles_per_seq
    xe = jnp.concatenate([xp_ref[...], xc_ref[...], xn_ref[...]], axis=0)
    da = jnp.concatenate([dap_ref[...], dac_ref[...], dan_ref[...]], axis=0)
    ft = jnp.concatenate([ftp_ref[...], ftc_ref[...], ftn_ref[...]], axis=0)
    x1 = xe + _dot(da, wo_ref[0:DA_WIDTH, :]) + _dot(ft, wo_ref[DA_WIDTH:, :])
    h2 = _rms(x1, gf_ref[...], EPS)
    r = jax.lax.broadcasted_iota(jnp.int32, (TM_FFN + 2 * HALO, 1), 0)
    valid = jnp.logical_and(jnp.logical_or(r >= HALO, pos > 0),
                            jnp.logical_or(r < TM_FFN + HALO, pos < tiles_per_seq - 1))
    h2 = jnp.where(valid, h2, 0.0).astype(BF16)
    n_ext = TM_FFN + 2 * HALO

    def conv(up, col0):
        w = wcv_ref[:, col0:col0 + FF_CHUNK]
        prev = pltpu.roll(up, 1, 0)[HALO:HALO + TM_FFN]
        nxt = pltpu.roll(up, n_ext - 1, 0)[HALO:HALO + TM_FFN]
        cur = up[HALO:HALO + TM_FFN]
        return (w[0:1] * prev + w[1:2] * cur + w[2:3] * nxt
                + bcv_ref[:, col0:col0 + FF_CHUNK])

    for j in range(D_FF // FF_CHUNK):
        c0 = j * FF_CHUNK
        gate = conv(_dot(h2, wup_ref[:, c0:c0 + FF_CHUNK]), c0)
        val = conv(_dot(h2, wup_ref[:, D_FF + c0:D_FF + c0 + FF_CHUNK]), D_FF + c0)
        act = gate / (1.0 + jnp.exp(-gate)) * val
        act_ref[:, c0:c0 + FF_CHUNK] = act.astype(BF16)

    y = x1[HALO:HALO + TM_FFN] + _dot(act_ref[...], wdn_ref[...])
    o_ref[...] = _rms(y, gl_ref[...], EPS)


def _ffn(x2, o_da, o_ft, w_out, g_ffn, w_up, w_conv, b_conv, w_down, g_final):
    n = BATCH * SEQ // TM_FFN
    hb = TM_FFN // HALO
    last = BATCH * SEQ // HALO - 1
    cur = lambda i: (i, 0)
    prv = lambda i: (jnp.maximum(i * hb - 1, 0), 0)
    nxt = lambda i: (jnp.minimum((i + 1) * hb, last), 0)
    const = lambda i: (0, 0)

    def trio(width):
        return [pl.BlockSpec((TM_FFN, width), cur), pl.BlockSpec((HALO, width), prv),
                pl.BlockSpec((HALO, width), nxt)]

    def resident(shape):
        return pl.BlockSpec(shape, const, pipeline_mode=pl.Buffered(1))

    return pl.pallas_call(
        _ffn_kernel,
        grid=(n,),
        in_specs=trio(D_MODEL) + trio(DA_WIDTH) + trio(FT_WIDTH) + [
            resident((D_MODEL, D_MODEL)),
            resident((1, D_MODEL)),
            resident((D_MODEL, 2 * D_FF)),
            resident((3, 2 * D_FF)),
            resident((1, 2 * D_FF)),
            resident((D_FF, D_MODEL)),
            resident((1, D_MODEL)),
        ],
        out_specs=pl.BlockSpec((TM_FFN, D_MODEL), cur),
        out_shape=jax.ShapeDtypeStruct((BATCH * SEQ, D_MODEL), F32),
        scratch_shapes=[pltpu.VMEM((TM_FFN, D_FF), BF16)],
        compiler_params=pltpu.CompilerParams(
            dimension_semantics=("arbitrary",), vmem_limit_bytes=VMEM_LIMIT),
        name="out_proj_ffn",
    )(x2, x2, x2, o_da, o_da, o_da, o_ft, o_ft, o_ft,
      w_out, g_ffn, w_up, w_conv, b_conv, w_down, g_final)


@functools.lru_cache(maxsize=None)
def _dft_tables():
    i64 = np.arange(64, dtype=np.int64)
    a1 = 2.0 * np.pi * ((i64[:, None] * i64[None, :]) % 64) / 64.0
    c1, s1 = np.cos(a1), np.sin(a1)
    w1 = np.block([[c1, -s1], [-s1, -c1]])
    kk = i64[:, None, None] + 64 * i64[None, :, None]
    a2 = 2.0 * np.pi * ((kk * i64[None, None, :]) % SEQ) / SEQ
    norm = 1.0 / math.sqrt(SEQ)
    m2 = np.concatenate([np.cos(a2) * norm, np.sin(a2) * norm], axis=2)
    c = np.arange(FT_GROUP_DIM, dtype=np.int64)
    ac = 2.0 * np.pi * ((c[:, None] * c[None, :]) % FT_GROUP_DIM) / FT_GROUP_DIM
    cn = 1.0 / math.sqrt(FT_GROUP_DIM)
    cs = np.concatenate([np.cos(ac) * cn, np.sin(ac) * cn], axis=1)
    f = lambda a: np.asarray(a, dtype=np.float32)
    return f(w1), f(m2), f(cs)


def kernel(x, g_mix, w_in, lambda_q1, lambda_k1, lambda_q2, lambda_k2, g_subln, w_ft, w_out,
           g_ffn, w_up, w_conv, b_conv, w_down, g_final):
    x2 = x.reshape(BATCH * SEQ, D_MODEL)
    w1, m2, cs = _dft_tables()
    qt, kz, vt, xcs = _in_proj(x2, g_mix[0][None, :], w_in[0].astype(BF16),
                               jnp.asarray(cs).astype(BF16), w_ft[0].astype(BF16))

    ch = jnp.asarray([LOG2E * 2.0 ** (-8.0 * (h + 1) / DA_HEADS) for h in range(DA_HEADS)], F32)
    o_da = _attention(ch, lambda_q1, lambda_k1, lambda_q2, lambda_k2,
                      g_subln[0][:, None], qt, kz, vt)

    o_ft = _seq_dft(xcs, jnp.asarray(w1).astype(BF16), jnp.asarray(m2).astype(BF16))

    y = _ffn(x2, o_da, o_ft, w_out[0].astype(BF16), g_ffn[0][None, :], w_up[0].astype(BF16),
             w_conv[0], b_conv[0][None, :], w_down[0].astype(BF16), g_final[None, :])
    return y.reshape(BATCH, SEQ, D_MODEL)
```

```python
import functools
import math

import numpy as np
import jax
import jax.numpy as jnp
from jax.experimental import pallas as pl
from jax.experimental.pallas import tpu as pltpu

D_MODEL = 1024
BATCH = 8
SEQ = 4096
DA_HEADS = 4
DA_HEAD_DIM = 64
DA_V_DIM = 128
QK_WIDTH = 512
DA_WIDTH = 512
FT_GROUPS = 4
FT_GROUP_DIM = 128
FT_WIDTH = 512
IN_PROJ_WIDTH = 2048
D_FF = 2816
EPS = 1e-6
SUBLN_EPS = 1e-5
LAMBDA_INIT = 0.8 - 0.6 * math.exp(-0.3 * 0)
LOG2E = 1.4426950408889634

BF16 = jnp.bfloat16
F32 = jnp.float32

TM_IN = 1024
TQ = 256
CH = TM_IN
V_ROWS = DA_V_DIM + 16
DFT_COLS = 256
P1 = 72
P2 = 136
TM_FFN = 1024
HALO = 16
FF_CHUNK = 256
VMEM_LIMIT = 56 * 1024 * 1024


def _dot(a, b):
    return jnp.dot(a, b, preferred_element_type=F32)


def _in_proj_kernel(x_ref, g_ref, w_ref, cs_ref, wft_ref, qt_ref, k_ref, vt_ref, xcs_ref):
    x = x_ref[...]
    ms = jnp.mean(x * x, axis=-1, keepdims=True)
    h = (x * jax.lax.rsqrt(ms + EPS) * g_ref[...]).astype(BF16)
    z = _dot(h, w_ref[...])
    row = jax.lax.broadcasted_iota(jnp.int32, (2 * DA_HEAD_DIM, TM_IN), 0)
    for hd in range(DA_HEADS):
        zq_t = (z[:, hd * 128:(hd + 1) * 128] * (DA_HEAD_DIM ** -0.5 * LOG2E)).T
        q1 = jnp.where(row < DA_HEAD_DIM, zq_t, 0.0).astype(BF16)
        q2 = jnp.where(row >= DA_HEAD_DIM, zq_t, 0.0).astype(BF16)
        for j in range(TM_IN // TQ):
            qt_ref[0, hd, j, 0] = q1[:, j * TQ:(j + 1) * TQ]
            qt_ref[0, hd, j, 1] = q2[:, j * TQ:(j + 1) * TQ]
        zv_t = z[:, 2 * QK_WIDTH + hd * 128:2 * QK_WIDTH + (hd + 1) * 128].T
        vt_ref[0, hd, 0, 0:DA_V_DIM, :] = zv_t.astype(BF16)
        ones_row = jax.lax.broadcasted_iota(jnp.int32, (V_ROWS - DA_V_DIM, TM_IN), 0) == 0
        vt_ref[0, hd, 0, DA_V_DIM:V_ROWS, :] = jnp.where(ones_row, 1.0, 0.0).astype(BF16)
    k_ref[...] = z[:, QK_WIDTH:2 * QK_WIDTH].astype(BF16)
    u0 = 2 * QK_WIDTH + DA_WIDTH
    for g in range(FT_GROUPS):
        ug = z[:, u0 + g * 128:u0 + (g + 1) * 128].astype(BF16)
        wg = wft_ref[g]
        csw = jnp.concatenate([_dot(cs_ref[:, :128], wg), _dot(cs_ref[:, 128:], wg)],
                              axis=1).astype(BF16)
        xcs = _dot(ug, csw)
        xcs_ref[0, 0, :, g * 128:(g + 1) * 128] = xcs[:, :128].astype(BF16)
        xcs_ref[0, 1, :, g * 128:(g + 1) * 128] = xcs[:, 128:].astype(BF16)


def _in_proj(x2, g_mix, w_in, cs, w_ft):
    nt = SEQ // TM_IN
    return pl.pallas_call(
        _in_proj_kernel,
        grid=(BATCH, nt),
        in_specs=[
            pl.BlockSpec((TM_IN, D_MODEL), lambda b, t: (b * nt + t, 0)),
            pl.BlockSpec((1, D_MODEL), lambda b, t: (0, 0)),
            pl.BlockSpec((D_MODEL, IN_PROJ_WIDTH), lambda b, t: (0, 0)),
            pl.BlockSpec((FT_GROUP_DIM, 2 * FT_GROUP_DIM), lambda b, t: (0, 0)),
            pl.BlockSpec((FT_GROUPS, FT_GROUP_DIM, FT_GROUP_DIM), lambda b, t: (0, 0, 0)),
        ],
        out_specs=[
            pl.BlockSpec((1, DA_HEADS, TM_IN // TQ, 2, 128, TQ),
                         lambda b, t: (b, 0, t, 0, 0, 0)),
            pl.BlockSpec((TM_IN, QK_WIDTH), lambda b, t: (b * nt + t, 0)),
            pl.BlockSpec((1, DA_HEADS, 1, V_ROWS, TM_IN), lambda b, t: (b, 0, t, 0, 0)),
            pl.BlockSpec((1, 2, TM_IN, FT_WIDTH), lambda b, t: (b, 0, t, 0)),
        ],
        out_shape=[
            jax.ShapeDtypeStruct((BATCH, DA_HEADS, SEQ // TQ, 2, 128, TQ), BF16),
            jax.ShapeDtypeStruct((BATCH * SEQ, QK_WIDTH), BF16),
            jax.ShapeDtypeStruct((BATCH, DA_HEADS, nt, V_ROWS, TM_IN), BF16),
            jax.ShapeDtypeStruct((BATCH, 2, SEQ, FT_WIDTH), BF16),
        ],
        compiler_params=pltpu.CompilerParams(
            dimension_semantics=("arbitrary", "arbitrary"), vmem_limit_bytes=VMEM_LIMIT),
        name="in_proj",
    )(x2, g_mix, w_in, cs, w_ft)


def _attn_kernel(ch_ref, lq1_ref, lk1_ref, lq2_ref, lk2_ref, gs_ref,
                 qt_ref, k_ref, vt_ref, o_ref, bias_ref, sa_ref, sb_ref, ma_ref, mb_ref,
                 acca_ref, accb_ref):
    hd = pl.program_id(0)
    bufs = ((sa_ref, ma_ref, acca_ref), (sb_ref, mb_ref, accb_ref))

    @pl.when(pl.program_id(1) == 0)
    def _build_bias():
        nslope = -ch_ref[hd]
        d = (jax.lax.broadcasted_iota(jnp.int32, (TQ, TQ), 0)
             - jax.lax.broadcasted_iota(jnp.int32, (TQ, TQ), 1))

        def rows(i, carry):
            off = i * TQ - (SEQ - TQ)
            bias_ref[pl.ds(pl.multiple_of(i * TQ, TQ), TQ), :] = (
                jnp.abs(d + off).astype(F32) * nslope)
            return carry

        jax.lax.fori_loop(0, (2 * SEQ - TQ) // TQ, rows, 0)

    nch = SEQ // CH
    nq = SEQ // TQ

    def rows_of(r):
        return slice(r * CH, (r + 1) * CH)

    def scores_chunk(u, r, par):
        s_ref, m_ref, _ = bufs[par]
        k_r = k_ref[rows_of(r), :]
        r0 = pl.multiple_of(r * CH - u * TQ + (SEQ - TQ), 128)
        bias = bias_ref[pl.ds(r0, CH), :]
        for c in range(2):
            s = _dot(k_r, qt_ref[0, 0, u, c]) + bias
            s_ref[c, rows_of(r), :] = s
            m_ref[c] = jnp.maximum(m_ref[c], jnp.max(s, axis=0, keepdims=True))

    def softmax_chunk(r, par):
        s_ref, m_ref, acc_ref = bufs[par]
        for c in range(2):
            p = jnp.exp2(s_ref[c, rows_of(r), :] - m_ref[c]).astype(BF16)
            acc_ref[c] += _dot(vt_ref[0, 0, r], p)

    def finalize(u, par):
        acc_ref = bufs[par][2]
        lam = (jnp.exp(jnp.sum(lq1_ref[...] * lk1_ref[...], axis=-1, keepdims=True))
               - jnp.exp(jnp.sum(lq2_ref[...] * lk2_ref[...], axis=-1, keepdims=True))
               + LAMBDA_INIT)
        a1 = acc_ref[0]
        a2 = acc_ref[1]
        o1 = a1[:DA_V_DIM] / a1[DA_V_DIM:DA_V_DIM + 1]
        o2 = a2[:DA_V_DIM] / a2[DA_V_DIM:DA_V_DIM + 1]
        o = o1 - lam * o2
        ms = jnp.mean(o * o, axis=0, keepdims=True)
        y = o * jax.lax.rsqrt(ms + SUBLN_EPS) * gs_ref[...] * (1.0 - LAMBDA_INIT)
        o_ref[pl.ds(pl.multiple_of(u * TQ, TQ), TQ), :] = y.T.astype(BF16)
        acc_ref[...] = jnp.zeros(acc_ref.shape, F32)

    def step(u, par, scores=True, softmax=True, epilogue=True):
        if epilogue:
            finalize(u - 2, par)
        if scores:
            bufs[par][1][...] = jnp.full((2, 1, TQ), -1e30, F32)
        for r in range(nch):
            if scores:
                scores_chunk(u, r, par)
            if softmax:
                softmax_chunk(r, 1 - par)

    acca_ref[...] = jnp.zeros(acca_ref.shape, F32)
    accb_ref[...] = jnp.zeros(accb_ref.shape, F32)
    step(0, 0, softmax=False, epilogue=False)
    step(1, 1, epilogue=False)

    def step_pair(j, carry):
        step(2 * j + 2, 0)
        step(2 * j + 3, 1)
        return carry

    jax.lax.fori_loop(0, nq // 2 - 1, step_pair, 0)
    step(nq, 0, scores=False)
    finalize(nq - 1, 1)


def _attention(ch, lq1, lk1, lq2, lk2, gs_col, qt, kz, vt):
    nq = SEQ // TQ
    nch = SEQ // CH
    vec = pl.BlockSpec((1, DA_HEAD_DIM), lambda h, b: (0, 0))
    return pl.pallas_call(
        _attn_kernel,
        grid=(DA_HEADS, BATCH),
        in_specs=[
            pl.BlockSpec(memory_space=pltpu.SMEM),
            vec, vec, vec, vec,
            pl.BlockSpec((DA_V_DIM, 1), lambda h, b: (0, 0)),
            pl.BlockSpec((1, 1, nq, 2, 128, TQ), lambda h, b: (b, h, 0, 0, 0, 0)),
            pl.BlockSpec((SEQ, 128), lambda h, b: (b, h)),
            pl.BlockSpec((1, 1, nch, V_ROWS, CH), lambda h, b: (b, h, 0, 0, 0)),
        ],
        out_specs=pl.BlockSpec((SEQ, DA_V_DIM), lambda h, b: (b, h)),
        out_shape=jax.ShapeDtypeStruct((BATCH * SEQ, DA_WIDTH), BF16),
        scratch_shapes=[pltpu.VMEM((2 * SEQ - TQ, TQ), F32),
                        pltpu.VMEM((2, SEQ, TQ), F32), pltpu.VMEM((2, SEQ, TQ), F32),
                        pltpu.VMEM((2, 1, TQ), F32), pltpu.VMEM((2, 1, TQ), F32),
                        pltpu.VMEM((2, V_ROWS, TQ), F32), pltpu.VMEM((2, V_ROWS, TQ), F32)],
        compiler_params=pltpu.CompilerParams(
            dimension_semantics=("arbitrary", "arbitrary"), vmem_limit_bytes=VMEM_LIMIT),
        name="diff_attention",
    )(ch, lq1, lk1, lq2, lk2, gs_col, qt, kz, vt)


def _seq_dft_kernel(x_ref, w1_ref, m2_ref, o_ref, xs_ref, a_ref, y_ref):
    nl = DFT_COLS // 128

    def slab(v, l):
        return v[:, l * 128:(l + 1) * 128]

    for n1 in range(64):
        for p in range(2):
            v = x_ref[0, p, n1 * 64:(n1 + 1) * 64, :].astype(F32)
            for l in range(nl):
                xs_ref[p * nl + l, n1 * P1:n1 * P1 + 64, :] = slab(v, l)

    def stage1(n2, carry):
        parts = [jnp.concatenate([xs_ref[p * nl + l, pl.ds(n2, 64, stride=P1), :]
                                  for l in range(nl)], axis=1) for p in range(2)]
        d = jnp.concatenate(parts, axis=0).astype(BF16)
        a = _dot(w1_ref[...], d)
        row0 = pl.multiple_of(n2 * P2, 8)
        for l in range(nl):
            a_ref[l, pl.ds(row0, 128), :] = slab(a, l)
        return carry

    jax.lax.fori_loop(0, 64, stage1, 0, unroll=8)

    def stage2(k1, carry):
        parts = [jnp.concatenate([a_ref[l, pl.ds(k1 + 64 * p, 64, stride=P2), :]
                                  for l in range(nl)], axis=1) for p in range(2)]
        d = jnp.concatenate(parts, axis=0).astype(BF16)
        y = _dot(m2_ref[k1], d)
        row0 = pl.multiple_of(k1 * P1, 8)
        for l in range(nl):
            y_ref[l, pl.ds(row0, 64), :] = slab(y, l)
        return carry

    jax.lax.fori_loop(0, 64, stage2, 0, unroll=8)

    def stage3(k2, carry):
        row0 = pl.multiple_of(k2 * 64, 64)
        for l in range(nl):
            o_ref[pl.ds(row0, 64), l * 128:(l + 1) * 128] = (
                y_ref[l, pl.ds(k2, 64, stride=P1), :].astype(BF16))
        return carry

    jax.lax.fori_loop(0, 64, stage3, 0, unroll=8)


def _seq_dft(xcs, w1, m2):
    nh = FT_WIDTH // DFT_COLS
    nl = DFT_COLS // 128
    return pl.pallas_call(
        _seq_dft_kernel,
        grid=(BATCH, nh),
        in_specs=[
            pl.BlockSpec((1, 2, SEQ, DFT_COLS), lambda b, j: (b, 0, 0, j)),
            pl.BlockSpec((128, 128), lambda b, j: (0, 0)),
            pl.BlockSpec((64, 64, 128), lambda b, j: (0, 0, 0)),
        ],
        out_specs=pl.BlockSpec((SEQ, DFT_COLS), lambda b, j: (b, j)),
        out_shape=jax.ShapeDtypeStruct((BATCH * SEQ, FT_WIDTH), BF16),
        scratch_shapes=[pltpu.VMEM((2 * nl, 64 * P1, 128), F32),
                        pltpu.VMEM((nl, 64 * P2, 128), F32),
                        pltpu.VMEM((nl, 64 * P1, 128), F32)],
        compiler_params=pltpu.CompilerParams(
            dimension_semantics=("arbitrary", "arbitrary"), vmem_limit_bytes=VMEM_LIMIT),
        name="seq_dft",
    )(xcs, w1, m2)


def _rms(x, g, eps):
    ms = jnp.mean(x * x, axis=-1, keepdims=True)
    return x * jax.lax.rsqrt(ms + eps) * g


def _ffn_kernel(xc_ref, xp_ref, xn_ref, dac_ref, dap_ref, dan_ref, ftc_ref, ftp_ref, ftn_ref,
                wo_ref, gf_ref, wup_ref, wcv_ref, bcv_ref, wdn_ref, gl_ref, o_ref, act_ref):
    i = pl.program_id(0)
    tiles_per_seq = SEQ // TM_FFN
    pos = i % tiles_per_seq
    xe = jnp.concatenate([xp_ref[...], xc_ref[...], xn_ref[...]], axis=0)
    da = jnp.concatenate([dap_ref[...], dac_ref[...], dan_ref[...]], axis=0)
    ft = jnp.concatenate([ftp_ref[...], ftc_ref[...], ftn_ref[...]], axis=0)
    x1 = xe + _dot(da, wo_ref[0:DA_WIDTH, :]) + _dot(ft, wo_ref[DA_WIDTH:, :])
    h2 = _rms(x1, gf_ref[...], EPS)
    r = jax.lax.broadcasted_iota(jnp.int32, (TM_FFN + 2 * HALO, 1), 0)
    valid = jnp.logical_and(jnp.logical_or(r >= HALO, pos > 0),
                            jnp.logical_or(r < TM_FFN + HALO, pos < tiles_per_seq - 1))
    h2 = jnp.where(valid, h2, 0.0).astype(BF16)
    n_ext = TM_FFN + 2 * HALO

    def conv(up, col0):
        w = wcv_ref[:, col0:col0 + FF_CHUNK]
        prev = pltpu.roll(up, 1, 0)[HALO:HALO + TM_FFN]
        nxt = pltpu.roll(up, n_ext - 1, 0)[HALO:HALO + TM_FFN]
        cur = up[HALO:HALO + TM_FFN]
        return (w[0:1] * prev + w[1:2] * cur + w[2:3] * nxt
                + bcv_ref[:, col0:col0 + FF_CHUNK])

    for j in range(D_FF // FF_CHUNK):
        c0 = j * FF_CHUNK
        gate = conv(_dot(h2, wup_ref[:, c0:c0 + FF_CHUNK]), c0)
        val = conv(_dot(h2, wup_ref[:, D_FF + c0:D_FF + c0 + FF_CHUNK]), D_FF + c0)
        act = gate / (1.0 + jnp.exp(-gate)) * val
        act_ref[:, c0:c0 + FF_CHUNK] = act.astype(BF16)

    y = x1[HALO:HALO + TM_FFN] + _dot(act_ref[...], wdn_ref[...])
    o_ref[...] = _rms(y, gl_ref[...], EPS)


def _ffn(x2, o_da, o_ft, w_out, g_ffn, w_up, w_conv, b_conv, w_down, g_final):
    n = BATCH * SEQ // TM_FFN
    hb = TM_FFN // HALO
    last = BATCH * SEQ // HALO - 1
    cur = lambda i: (i, 0)
    prv = lambda i: (jnp.maximum(i * hb - 1, 0), 0)
    nxt = lambda i: (jnp.minimum((i + 1) * hb, last), 0)
    const = lambda i: (0, 0)

    def trio(width):
        return [pl.BlockSpec((TM_FFN, width), cur), pl.BlockSpec((HALO, width), prv),
                pl.BlockSpec((HALO, width), nxt)]

    def resident(shape):
        return pl.BlockSpec(shape, const, pipeline_mode=pl.Buffered(1))

    return pl.pallas_call(
        _ffn_kernel,
        grid=(n,),
        in_specs=trio(D_MODEL) + trio(DA_WIDTH) + trio(FT_WIDTH) + [
            resident((D_MODEL, D_MODEL)),
            resident((1, D_MODEL)),
            resident((D_MODEL, 2 * D_FF)),
            resident((3, 2 * D_FF)),
            resident((1, 2 * D_FF)),
            resident((D_FF, D_MODEL)),
            resident((1, D_MODEL)),
        ],
        out_specs=pl.BlockSpec((TM_FFN, D_MODEL), cur),
        out_shape=jax.ShapeDtypeStruct((BATCH * SEQ, D_MODEL), F32),
        scratch_shapes=[pltpu.VMEM((TM_FFN, D_FF), BF16)],
        compiler_params=pltpu.CompilerParams(
            dimension_semantics=("arbitrary",), vmem_limit_bytes=VMEM_LIMIT),
        name="out_proj_ffn",
    )(x2, x2, x2, o_da, o_da, o_da, o_ft, o_ft, o_ft,
      w_out, g_ffn, w_up, w_conv, b_conv, w_down, g_final)


@functools.lru_cache(maxsize=None)
def _dft_tables():
    i64 = np.arange(64, dtype=np.int64)
    a1 = 2.0 * np.pi * ((i64[:, None] * i64[None, :]) % 64) / 64.0
    c1, s1 = np.cos(a1), np.sin(a1)
    w1 = np.block([[c1, -s1], [-s1, -c1]])
    kk = i64[:, None, None] + 64 * i64[None, :, None]
    a2 = 2.0 * np.pi * ((kk * i64[None, None, :]) % SEQ) / SEQ
    norm = 1.0 / math.sqrt(SEQ)
    m2 = np.concatenate([np.cos(a2) * norm, np.sin(a2) * norm], axis=2)
    c = np.arange(FT_GROUP_DIM, dtype=np.int64)
    ac = 2.0 * np.pi * ((c[:, None] * c[None, :]) % FT_GROUP_DIM) / FT_GROUP_DIM
    cn = 1.0 / math.sqrt(FT_GROUP_DIM)
    cs = np.concatenate([np.cos(ac) * cn, np.sin(ac) * cn], axis=1)
    f = lambda a: np.asarray(a, dtype=np.float32)
    return f(w1), f(m2), f(cs)


def kernel(x, g_mix, w_in, lambda_q1, lambda_k1, lambda_q2, lambda_k2, g_subln, w_ft, w_out,
           g_ffn, w_up, w_conv, b_conv, w_down, g_final):
    x2 = x.reshape(BATCH * SEQ, D_MODEL)
    w1, m2, cs = _dft_tables()
    qt, kz, vt, xcs = _in_proj(x2, g_mix[0][None, :], w_in[0].astype(BF16),
                               jnp.asarray(cs).astype(BF16), w_ft[0].astype(BF16))

    ch = jnp.asarray([LOG2E * 2.0 ** (-8.0 * (h + 1) / DA_HEADS) for h in range(DA_HEADS)], F32)
    o_da = _attention(ch, lambda_q1, lambda_k1, lambda_q2, lambda_k2,
                      g_subln[0][:, None], qt, kz, vt)

    o_ft = _seq_dft(xcs, jnp.asarray(w1).astype(BF16), jnp.asarray(m2).astype(BF16))

    y = _ffn(x2, o_da, o_ft, w_out[0].astype(BF16), g_ffn[0][None, :], w_up[0].astype(BF16),
             w_conv[0], b_conv[0][None, :], w_down[0].astype(BF16), g_final[None, :])
    return y.reshape(BATCH, SEQ, D_MODEL)
```

```python
import functools
import math

import numpy as np
import jax
import jax.numpy as jnp
from jax.experimental import pallas as pl
from jax.experimental.pallas import tpu as pltpu

D_MODEL = 1024
BATCH = 8
SEQ = 4096
DA_HEADS = 4
DA_HEAD_DIM = 64
DA_V_DIM = 128
QK_WIDTH = 512
DA_WIDTH = 512
FT_GROUPS = 4
FT_GROUP_DIM = 128
FT_WIDTH = 512
IN_PROJ_WIDTH = 2048
D_FF = 2816
EPS = 1e-6
SUBLN_EPS = 1e-5
LAMBDA_INIT = 0.8 - 0.6 * math.exp(-0.3 * 0)
LOG2E = 1.4426950408889634

BF16 = jnp.bfloat16
F32 = jnp.float32

TM_IN = 1024
TQ = 256
CH = TM_IN
CORR_ROWS = CH + (CH // TQ - 1) * TQ
N_SPLIT = 4
V_ROWS = DA_V_DIM + 16
DFT_COLS = 256
P1 = 72
P2 = 136
TM_FFN = 1024
HALO = 16
FF_CHUNK = 256
VMEM_LIMIT = 56 * 1024 * 1024


def _dot(a, b):
    return jnp.dot(a, b, preferred_element_type=F32)


def _in_proj_kernel(x_ref, g_ref, w_ref, cs_ref, wft_ref, qt_ref, k_ref, vt_ref, xcs_ref):
    x = x_ref[...]
    ms = jnp.mean(x * x, axis=-1, keepdims=True)
    h = (x * jax.lax.rsqrt(ms + EPS) * g_ref[...]).astype(BF16)
    z = _dot(h, w_ref[...])
    row = jax.lax.broadcasted_iota(jnp.int32, (2 * DA_HEAD_DIM, TM_IN), 0)
    for hd in range(DA_HEADS):
        zq_t = (z[:, hd * 128:(hd + 1) * 128] * (DA_HEAD_DIM ** -0.5 * LOG2E)).T
        q1 = jnp.where(row < DA_HEAD_DIM, zq_t, 0.0).astype(BF16)
        q2 = jnp.where(row >= DA_HEAD_DIM, zq_t, 0.0).astype(BF16)
        for j in range(TM_IN // TQ):
            qt_ref[0, hd, j, 0] = q1[:, j * TQ:(j + 1) * TQ]
            qt_ref[0, hd, j, 1] = q2[:, j * TQ:(j + 1) * TQ]
        zv_t = z[:, 2 * QK_WIDTH + hd * 128:2 * QK_WIDTH + (hd + 1) * 128].T
        vt_ref[0, hd, 0, 0:DA_V_DIM, :] = zv_t.astype(BF16)
        ones_row = jax.lax.broadcasted_iota(jnp.int32, (V_ROWS - DA_V_DIM, TM_IN), 0) == 0
        vt_ref[0, hd, 0, DA_V_DIM:V_ROWS, :] = jnp.where(ones_row, 1.0, 0.0).astype(BF16)
    k_ref[...] = z[:, QK_WIDTH:2 * QK_WIDTH].astype(BF16)
    u0 = 2 * QK_WIDTH + DA_WIDTH
    for g in range(FT_GROUPS):
        ug = z[:, u0 + g * 128:u0 + (g + 1) * 128].astype(BF16)
        wg = wft_ref[g]
        csw = jnp.concatenate([_dot(cs_ref[:, :128], wg), _dot(cs_ref[:, 128:], wg)],
                              axis=1).astype(BF16)
        xcs = _dot(ug, csw)
        xcs_ref[0, 0, :, g * 128:(g + 1) * 128] = xcs[:, :128].astype(BF16)
        xcs_ref[0, 1, :, g * 128:(g + 1) * 128] = xcs[:, 128:].astype(BF16)


def _in_proj(x2, g_mix, w_in, cs, w_ft):
    nt = SEQ // TM_IN
    return pl.pallas_call(
        _in_proj_kernel,
        grid=(BATCH, nt),
        in_specs=[
            pl.BlockSpec((TM_IN, D_MODEL), lambda b, t: (b * nt + t, 0)),
            pl.BlockSpec((1, D_MODEL), lambda b, t: (0, 0)),
            pl.BlockSpec((D_MODEL, IN_PROJ_WIDTH), lambda b, t: (0, 0)),
            pl.BlockSpec((FT_GROUP_DIM, 2 * FT_GROUP_DIM), lambda b, t: (0, 0)),
            pl.BlockSpec((FT_GROUPS, FT_GROUP_DIM, FT_GROUP_DIM), lambda b, t: (0, 0, 0)),
        ],
        out_specs=[
            pl.BlockSpec((1, DA_HEADS, TM_IN // TQ, 2, 128, TQ),
                         lambda b, t: (b, 0, t, 0, 0, 0)),
            pl.BlockSpec((TM_IN, QK_WIDTH), lambda b, t: (b * nt + t, 0)),
            pl.BlockSpec((1, DA_HEADS, 1, V_ROWS, TM_IN), lambda b, t: (b, 0, t, 0, 0)),
            pl.BlockSpec((1, 2, TM_IN, FT_WIDTH), lambda b, t: (b, 0, t, 0)),
        ],
        out_shape=[
            jax.ShapeDtypeStruct((BATCH, DA_HEADS, SEQ // TQ, 2, 128, TQ), BF16),
            jax.ShapeDtypeStruct((BATCH * SEQ, QK_WIDTH), BF16),
            jax.ShapeDtypeStruct((BATCH, DA_HEADS, nt, V_ROWS, TM_IN), BF16),
            jax.ShapeDtypeStruct((BATCH, 2, SEQ, FT_WIDTH), BF16),
        ],
        compiler_params=pltpu.CompilerParams(
            dimension_semantics=("arbitrary", "arbitrary"), vmem_limit_bytes=VMEM_LIMIT),
        name="in_proj",
    )(x2, g_mix, w_in, cs, w_ft)


def _attn_kernel(ch_ref, lq1_ref, lk1_ref, lq2_ref, lk2_ref, gs_ref,
                 qt_ref, qaux_ref, k_ref, kaux_ref, vt_ref, o_ref, corr_ref,
                 sa_ref, sb_ref, ma_ref, mb_ref, acca_ref, accb_ref):
    hd = pl.program_id(0)
    bufs = ((sa_ref, ma_ref, acca_ref), (sb_ref, mb_ref, accb_ref))
    nch = SEQ // CH
    nq = SEQ // TQ
    tiles_per_chunk = CH // TQ

    @pl.when(pl.program_id(1) == 0)
    def _build_corr():
        n2slope = -2.0 * ch_ref[hd]
        d = (jax.lax.broadcasted_iota(jnp.int32, (TQ, TQ), 0)
             - jax.lax.broadcasted_iota(jnp.int32, (TQ, TQ), 1))
        for i in range(CORR_ROWS // TQ):
            off = (i - (tiles_per_chunk - 1)) * TQ
            corr_ref[i * TQ:(i + 1) * TQ, :] = jnp.maximum(d + off, 0).astype(F32) * n2slope

    def rows_of(r):
        return slice(r * CH, (r + 1) * CH)

    def scores_chunk(u, t, par):
        s_ref, m_ref, _ = bufs[par]
        rd = u // tiles_per_chunk
        r = (rd + t) % nch
        rows = pl.ds(pl.multiple_of(r * CH, CH), CH)
        lhs = jnp.concatenate([k_ref[rows, :], kaux_ref[0, rows, :]], axis=1)
        after = 0 if t == 0 else jnp.asarray(r > rd, jnp.int32)
        aux = qaux_ref[0, u, after]
        for c in range(2):
            s = _dot(lhs, jnp.concatenate([qt_ref[0, 0, u, c], aux], axis=0))
            if t == 0:
                start = pl.multiple_of((tiles_per_chunk - 1 - u % tiles_per_chunk) * TQ, TQ)
                s = s + corr_ref[pl.ds(start, CH), :]
            s_ref[c, rows, :] = s
            m_ref[c] = jnp.maximum(m_ref[c], jnp.max(s, axis=0, keepdims=True))

    def softmax_chunk(r, par):
        s_ref, m_ref, acc_ref = bufs[par]
        for c in range(2):
            p = jnp.exp2(s_ref[c, rows_of(r), :] - m_ref[c]).astype(BF16)
            acc_ref[c] += _dot(vt_ref[0, 0, r], p)

    def finalize(u, par):
        acc_ref = bufs[par][2]
        lam = (jnp.exp(jnp.sum(lq1_ref[...] * lk1_ref[...], axis=-1, keepdims=True))
               - jnp.exp(jnp.sum(lq2_ref[...] * lk2_ref[...], axis=-1, keepdims=True))
               + LAMBDA_INIT)
        a1 = acc_ref[0]
        a2 = acc_ref[1]
        o1 = a1[:DA_V_DIM] / a1[DA_V_DIM:DA_V_DIM + 1]
        o2 = a2[:DA_V_DIM] / a2[DA_V_DIM:DA_V_DIM + 1]
        o = o1 - lam * o2
        ms = jnp.mean(o * o, axis=0, keepdims=True)
        y = o * jax.lax.rsqrt(ms + SUBLN_EPS) * gs_ref[...] * (1.0 - LAMBDA_INIT)
        o_ref[pl.ds(pl.multiple_of(u * TQ, TQ), TQ), :] = y.T.astype(BF16)
        acc_ref[...] = jnp.zeros(acc_ref.shape, F32)

    def step(u, par, scores=True, softmax=True, epilogue=True):
        if epilogue:
            finalize(u - 2, par)
        if scores:
            bufs[par][1][...] = jnp.full((2, 1, TQ), -1e30, F32)
        for r in range(nch):
            if scores:
                scores_chunk(u, r, par)
            if softmax:
                softmax_chunk(r, 1 - par)

    acca_ref[...] = jnp.zeros(acca_ref.shape, F32)
    accb_ref[...] = jnp.zeros(accb_ref.shape, F32)
    step(0, 0, softmax=False, epilogue=False)
    step(1, 1, epilogue=False)

    def step_pair(j, carry):
        step(2 * j + 2, 0)
        step(2 * j + 3, 1)
        return carry

    jax.lax.fori_loop(0, nq // 2 - 1, step_pair, 0)
    step(nq, 0, scores=False)
    finalize(nq - 1, 1)


def _attention(ch, lq1, lk1, lq2, lk2, gs_col, qt, qaux, kz, kaux, vt):
    nq = SEQ // TQ
    nch = SEQ // CH
    vec = pl.BlockSpec((1, DA_HEAD_DIM), lambda h, b: (0, 0))
    return pl.pallas_call(
        _attn_kernel,
        grid=(DA_HEADS, BATCH),
        in_specs=[
            pl.BlockSpec(memory_space=pltpu.SMEM),
            vec, vec, vec, vec,
            pl.BlockSpec((DA_V_DIM, 1), lambda h, b: (0, 0)),
            pl.BlockSpec((1, 1, nq, 2, 128, TQ), lambda h, b: (b, h, 0, 0, 0, 0)),
            pl.BlockSpec((1, nq, 2, 128, TQ), lambda h, b: (h, 0, 0, 0, 0)),
            pl.BlockSpec((SEQ, 128), lambda h, b: (b, h)),
            pl.BlockSpec((1, SEQ, 128), lambda h, b: (h, 0, 0)),
            pl.BlockSpec((1, 1, nch, V_ROWS, CH), lambda h, b: (b, h, 0, 0, 0)),
        ],
        out_specs=pl.BlockSpec((SEQ, DA_V_DIM), lambda h, b: (b, h)),
        out_shape=jax.ShapeDtypeStruct((BATCH * SEQ, DA_WIDTH), BF16),
        scratch_shapes=[pltpu.VMEM((CORR_ROWS, TQ), F32),
                        pltpu.VMEM((2, SEQ, TQ), F32), pltpu.VMEM((2, SEQ, TQ), F32),
                        pltpu.VMEM((2, 1, TQ), F32), pltpu.VMEM((2, 1, TQ), F32),
                        pltpu.VMEM((2, V_ROWS, TQ), F32), pltpu.VMEM((2, V_ROWS, TQ), F32)],
        compiler_params=pltpu.CompilerParams(
            dimension_semantics=("arbitrary", "arbitrary"), vmem_limit_bytes=VMEM_LIMIT),
        name="diff_attention",
    )(ch, lq1, lk1, lq2, lk2, gs_col, qt, qaux, kz, kaux, vt)


def _seq_dft_kernel(x_ref, w1_ref, m2_ref, o_ref, xs_ref, a_ref, y_ref):
    nl = DFT_COLS // 128

    def slab(v, l):
        return v[:, l * 128:(l + 1) * 128]

    for n1 in range(64):
        for p in range(2):
            v = x_ref[0, p, n1 * 64:(n1 + 1) * 64, :].astype(F32)
            for l in range(nl):
                xs_ref[p * nl + l, n1 * P1:n1 * P1 + 64, :] = slab(v, l)

    def stage1(n2, carry):
        parts = [jnp.concatenate([xs_ref[p * nl + l, pl.ds(n2, 64, stride=P1), :]
                                  for l in range(nl)], axis=1) for p in range(2)]
        d = jnp.concatenate(parts, axis=0).astype(BF16)
        a = _dot(w1_ref[...], d)
        row0 = pl.multiple_of(n2 * P2, 8)
        for l in range(nl):
            a_ref[l, pl.ds(row0, 128), :] = slab(a, l)
        return carry

    jax.lax.fori_loop(0, 64, stage1, 0, unroll=8)

    def stage2(k1, carry):
        parts = [jnp.concatenate([a_ref[l, pl.ds(k1 + 64 * p, 64, stride=P2), :]
                                  for l in range(nl)], axis=1) for p in range(2)]
        d = jnp.concatenate(parts, axis=0).astype(BF16)
        y = _dot(m2_ref[k1], d)
        row0 = pl.multiple_of(k1 * P1, 8)
        for l in range(nl):
            y_ref[l, pl.ds(row0, 64), :] = slab(y, l)
        return carry

    jax.lax.fori_loop(0, 64, stage2, 0, unroll=8)

    def stage3(k2, carry):
        row0 = pl.multiple_of(k2 * 64, 64)
        for l in range(nl):
            o_ref[pl.ds(row0, 64), l * 128:(l + 1) * 128] = (
                y_ref[l, pl.ds(k2, 64, stride=P1), :].astype(BF16))
        return carry

    jax.lax.fori_loop(0, 64, stage3, 0, unroll=8)


def _seq_dft(xcs, w1, m2):
    nh = FT_WIDTH // DFT_COLS
    nl = DFT_COLS // 128
    return pl.pallas_call(
        _seq_dft_kernel,
        grid=(BATCH, nh),
        in_specs=[
            pl.BlockSpec((1, 2, SEQ, DFT_COLS), lambda b, j: (b, 0, 0, j)),
            pl.BlockSpec((128, 128), lambda b, j: (0, 0)),
            pl.BlockSpec((64, 64, 128), lambda b, j: (0, 0, 0)),
        ],
        out_specs=pl.BlockSpec((SEQ, DFT_COLS), lambda b, j: (b, j)),
        out_shape=jax.ShapeDtypeStruct((BATCH * SEQ, FT_WIDTH), BF16),
        scratch_shapes=[pltpu.VMEM((2 * nl, 64 * P1, 128), F32),
                        pltpu.VMEM((nl, 64 * P2, 128), F32),
                        pltpu.VMEM((nl, 64 * P1, 128), F32)],
        compiler_params=pltpu.CompilerParams(
            dimension_semantics=("arbitrary", "arbitrary"), vmem_limit_bytes=VMEM_LIMIT),
        name="seq_dft",
    )(xcs, w1, m2)


def _rms(x, g, eps):
    ms = jnp.mean(x * x, axis=-1, keepdims=True)
    return x * jax.lax.rsqrt(ms + eps) * g


def _ffn_kernel(xc_ref, xp_ref, xn_ref, dac_ref, dap_ref, dan_ref, ftc_ref, ftp_ref, ftn_ref,
                wo_ref, gf_ref, wup_ref, wcv_ref, bcv_ref, wdn_ref, gl_ref, o_ref, act_ref):
    i = pl.program_id(0)
    tiles_per_seq = SEQ // TM_FFN
    pos = i % tiles_per_seq
    xe = jnp.concatenate([xp_ref[...], xc_ref[...], xn_ref[...]], axis=0)
    da = jnp.concatenate([dap_ref[...], dac_ref[...], dan_ref[...]], axis=0)
    ft = jnp.concatenate([ftp_ref[...], ftc_ref[...], ftn_ref[...]], axis=0)
    x1 = xe + _dot(da, wo_ref[0:DA_WIDTH, :]) + _dot(ft, wo_ref[DA_WIDTH:, :])
    h2 = _rms(x1, gf_ref[...], EPS)
    r = jax.lax.broadcasted_iota(jnp.int32, (TM_FFN + 2 * HALO, 1), 0)
    valid = jnp.logical_and(jnp.logical_or(r >= HALO, pos > 0),
                            jnp.logical_or(r < TM_FFN + HALO, pos < tiles_per_seq - 1))
    h2 = jnp.where(valid, h2, 0.0).astype(BF16)
    n_ext = TM_FFN + 2 * HALO

    def conv(up, col0):
        w = wcv_ref[:, col0:col0 + FF_CHUNK]
        prev = pltpu.roll(up, 1, 0)[HALO:HALO + TM_FFN]
        nxt = pltpu.roll(up, n_ext - 1, 0)[HALO:HALO + TM_FFN]
        cur = up[HALO:HALO + TM_FFN]
        return (w[0:1] * prev + w[1:2] * cur + w[2:3] * nxt
                + bcv_ref[:, col0:col0 + FF_CHUNK])

    for j in range(D_FF // FF_CHUNK):
        c0 = j * FF_CHUNK
        gate = conv(_dot(h2, wup_ref[:, c0:c0 + FF_CHUNK]), c0)
        val = conv(_dot(h2, wup_ref[:, D_FF + c0:D_FF + c0 + FF_CHUNK]), D_FF + c0)
        act = gate / (1.0 + jnp.exp(-gate)) * val
        act_ref[:, c0:c0 + FF_CHUNK] = act.astype(BF16)

    y = x1[HALO:HALO + TM_FFN] + _dot(act_ref[...], wdn_ref[...])
    o_ref[...] = _rms(y, gl_ref[...], EPS)


def _ffn(x2, o_da, o_ft, w_out, g_ffn, w_up, w_conv, b_conv, w_down, g_final):
    n = BATCH * SEQ // TM_FFN
    hb = TM_FFN // HALO
    last = BATCH * SEQ // HALO - 1
    cur = lambda i: (i, 0)
    prv = lambda i: (jnp.maximum(i * hb - 1, 0), 0)
    nxt = lambda i: (jnp.minimum((i + 1) * hb, last), 0)
    const = lambda i: (0, 0)

    def trio(width):
        return [pl.BlockSpec((TM_FFN, width), cur), pl.BlockSpec((HALO, width), prv),
                pl.BlockSpec((HALO, width), nxt)]

    def resident(shape):
        return pl.BlockSpec(shape, const, pipeline_mode=pl.Buffered(1))

    return pl.pallas_call(
        _ffn_kernel,
        grid=(n,),
        in_specs=trio(D_MODEL) + trio(DA_WIDTH) + trio(FT_WIDTH) + [
            resident((D_MODEL, D_MODEL)),
            resident((1, D_MODEL)),
            resident((D_MODEL, 2 * D_FF)),
            resident((3, 2 * D_FF)),
            resident((1, 2 * D_FF)),
            resident((D_FF, D_MODEL)),
            resident((1, D_MODEL)),
        ],
        out_specs=pl.BlockSpec((TM_FFN, D_MODEL), cur),
        out_shape=jax.ShapeDtypeStruct((BATCH * SEQ, D_MODEL), F32),
        scratch_shapes=[pltpu.VMEM((TM_FFN, D_FF), BF16)],
        compiler_params=pltpu.CompilerParams(
            dimension_semantics=("arbitrary",), vmem_limit_bytes=VMEM_LIMIT),
        name="out_proj_ffn",
    )(x2, x2, x2, o_da, o_da, o_da, o_ft, o_ft, o_ft,
      w_out, g_ffn, w_up, w_conv, b_conv, w_down, g_final)


@functools.lru_cache(maxsize=None)
def _dft_tables():
    i64 = np.arange(64, dtype=np.int64)
    a1 = 2.0 * np.pi * ((i64[:, None] * i64[None, :]) % 64) / 64.0
    c1, s1 = np.cos(a1), np.sin(a1)
    w1 = np.block([[c1, -s1], [-s1, -c1]])
    kk = i64[:, None, None] + 64 * i64[None, :, None]
    a2 = 2.0 * np.pi * ((kk * i64[None, None, :]) % SEQ) / SEQ
    norm = 1.0 / math.sqrt(SEQ)
    m2 = np.concatenate([np.cos(a2) * norm, np.sin(a2) * norm], axis=2)
    c = np.arange(FT_GROUP_DIM, dtype=np.int64)
    ac = 2.0 * np.pi * ((c[:, None] * c[None, :]) % FT_GROUP_DIM) / FT_GROUP_DIM
    cn = 1.0 / math.sqrt(FT_GROUP_DIM)
    cs = np.concatenate([np.cos(ac) * cn, np.sin(ac) * cn], axis=1)
    f = lambda a: np.asarray(a, dtype=np.float32)
    return f(w1), f(m2), f(cs)


def _slope_log2e(h):
    return np.float32(LOG2E * 2.0 ** (-8.0 * (h + 1) / DA_HEADS))


def _bf16_pieces(x, n):
    pieces, rem = [], np.asarray(x, dtype=np.float64)
    for _ in range(n):
        u = rem.astype(np.float32).view(np.uint32)
        u = (u + np.uint32(0x7FFF) + ((u >> np.uint32(16)) & np.uint32(1))) & np.uint32(0xFFFF0000)
        p = u.view(np.float32).astype(np.float64)
        pieces.append(p)
        rem = rem - p
    return pieces


@functools.lru_cache(maxsize=None)
def _alibi_tables():
    pos = np.arange(SEQ, dtype=np.float64)
    kaux = np.zeros((DA_HEADS, SEQ, 128), np.float64)
    qaux = np.zeros((DA_HEADS, 2, 128, SEQ), np.float64)
    for h in range(DA_HEADS):
        pieces = _bf16_pieces(np.float64(_slope_log2e(h)) * pos, N_SPLIT)
        for t, p in enumerate(pieces):
            kaux[h, :, t] = p
            qaux[h, 0, N_SPLIT + t, :] = -p
        kaux[h, :, N_SPLIT:2 * N_SPLIT] = 1.0
        qaux[h, 0, 0:N_SPLIT, :] = 1.0
        qaux[h, 1] = -qaux[h, 0]
    qaux = qaux.reshape(DA_HEADS, 2, 128, SEQ // TQ, TQ).transpose(0, 3, 1, 2, 4)
    return np.asarray(kaux, np.float32), np.ascontiguousarray(qaux, dtype=np.float32)


def kernel(x, g_mix, w_in, lambda_q1, lambda_k1, lambda_q2, lambda_k2, g_subln, w_ft, w_out,
           g_ffn, w_up, w_conv, b_conv, w_down, g_final):
    x2 = x.reshape(BATCH * SEQ, D_MODEL)
    w1, m2, cs = _dft_tables()
    qt, kz, vt, xcs = _in_proj(x2, g_mix[0][None, :], w_in[0].astype(BF16),
                               jnp.asarray(cs).astype(BF16), w_ft[0].astype(BF16))

    ch = jnp.asarray([_slope_log2e(h) for h in range(DA_HEADS)], F32)
    kaux, qaux = _alibi_tables()
    o_da = _attention(ch, lambda_q1, lambda_k1, lambda_q2, lambda_k2, g_subln[0][:, None],
                      qt, jnp.asarray(qaux).astype(BF16), kz, jnp.asarray(kaux).astype(BF16), vt)

    o_ft = _seq_dft(xcs, jnp.asarray(w1).astype(BF16), jnp.asarray(m2).astype(BF16))

    y = _ffn(x2, o_da, o_ft, w_out[0].astype(BF16), g_ffn[0][None, :], w_up[0].astype(BF16),
             w_conv[0], b_conv[0][None, :], w_down[0].astype(BF16), g_final[None, :])
    return y.reshape(BATCH, SEQ, D_MODEL)
```

```python
import functools
import math

import numpy as np
import jax
import jax.numpy as jnp
from jax.experimental import pallas as pl
from jax.experimental.pallas import tpu as pltpu

D_MODEL = 1024
BATCH = 8
SEQ = 4096
DA_HEADS = 4
DA_HEAD_DIM = 64
DA_V_DIM = 128
QK_WIDTH = 512
DA_WIDTH = 512
FT_GROUPS = 4
FT_GROUP_DIM = 128
FT_WIDTH = 512
IN_PROJ_WIDTH = 2048
D_FF = 2816
EPS = 1e-6
SUBLN_EPS = 1e-5
LAMBDA_INIT = 0.8 - 0.6 * math.exp(-0.3 * 0)
LOG2E = 1.4426950408889634

BF16 = jnp.bfloat16
F32 = jnp.float32

TM_IN = 1024
TQ = 256
CH = 256
ATT_BATCHES = 2
CORR_ROWS = CH + (CH // TQ - 1) * TQ
N_SPLIT = 4
V_ROWS = DA_V_DIM + 16
DFT_COLS = 256
P1 = 72
P2 = 136
TM_FFN = 1024
HALO = 16
FF_CHUNK = 256
VMEM_LIMIT = 56 * 1024 * 1024


def _dot(a, b):
    return jnp.dot(a, b, preferred_element_type=F32)


def _in_proj_kernel(x_ref, g_ref, w_ref, cs_ref, wft_ref, qt_ref, k_ref, vt_ref, xcs_ref):
    x = x_ref[...]
    ms = jnp.mean(x * x, axis=-1, keepdims=True)
    h = (x * jax.lax.rsqrt(ms + EPS) * g_ref[...]).astype(BF16)
    z = _dot(h, w_ref[...])
    row = jax.lax.broadcasted_iota(jnp.int32, (2 * DA_HEAD_DIM, TM_IN), 0)
    for hd in range(DA_HEADS):
        zq_t = (z[:, hd * 128:(hd + 1) * 128] * (DA_HEAD_DIM ** -0.5 * LOG2E)).T
        q1 = jnp.where(row < DA_HEAD_DIM, zq_t, 0.0).astype(BF16)
        q2 = jnp.where(row >= DA_HEAD_DIM, zq_t, 0.0).astype(BF16)
        for j in range(TM_IN // TQ):
            qt_ref[0, hd, j, 0] = q1[:, j * TQ:(j + 1) * TQ]
            qt_ref[0, hd, j, 1] = q2[:, j * TQ:(j + 1) * TQ]
        zv_t = z[:, 2 * QK_WIDTH + hd * 128:2 * QK_WIDTH + (hd + 1) * 128].T
        vt_ref[0, hd, 0, 0:DA_V_DIM, :] = zv_t.astype(BF16)
        ones_row = jax.lax.broadcasted_iota(jnp.int32, (V_ROWS - DA_V_DIM, TM_IN), 0) == 0
        vt_ref[0, hd, 0, DA_V_DIM:V_ROWS, :] = jnp.where(ones_row, 1.0, 0.0).astype(BF16)
    k_ref[...] = z[:, QK_WIDTH:2 * QK_WIDTH].astype(BF16)
    u0 = 2 * QK_WIDTH + DA_WIDTH
    for g in range(FT_GROUPS):
        ug = z[:, u0 + g * 128:u0 + (g + 1) * 128].astype(BF16)
        wg = wft_ref[g]
        csw = jnp.concatenate([_dot(cs_ref[:, :128], wg), _dot(cs_ref[:, 128:], wg)],
                              axis=1).astype(BF16)
        xcs = _dot(ug, csw)
        xcs_ref[0, 0, :, g * 128:(g + 1) * 128] = xcs[:, :128].astype(BF16)
        xcs_ref[0, 1, :, g * 128:(g + 1) * 128] = xcs[:, 128:].astype(BF16)


def _in_proj(x2, g_mix, w_in, cs, w_ft):
    nt = SEQ // TM_IN
    return pl.pallas_call(
        _in_proj_kernel,
        grid=(BATCH, nt),
        in_specs=[
            pl.BlockSpec((TM_IN, D_MODEL), lambda b, t: (b * nt + t, 0)),
            pl.BlockSpec((1, D_MODEL), lambda b, t: (0, 0)),
            pl.BlockSpec((D_MODEL, IN_PROJ_WIDTH), lambda b, t: (0, 0)),
            pl.BlockSpec((FT_GROUP_DIM, 2 * FT_GROUP_DIM), lambda b, t: (0, 0)),
            pl.BlockSpec((FT_GROUPS, FT_GROUP_DIM, FT_GROUP_DIM), lambda b, t: (0, 0, 0)),
        ],
        out_specs=[
            pl.BlockSpec((1, DA_HEADS, TM_IN // TQ, 2, 128, TQ),
                         lambda b, t: (b, 0, t, 0, 0, 0)),
            pl.BlockSpec((TM_IN, QK_WIDTH), lambda b, t: (b * nt + t, 0)),
            pl.BlockSpec((1, DA_HEADS, 1, V_ROWS, TM_IN), lambda b, t: (b, 0, t, 0, 0)),
            pl.BlockSpec((1, 2, TM_IN, FT_WIDTH), lambda b, t: (b, 0, t, 0)),
        ],
        out_shape=[
            jax.ShapeDtypeStruct((BATCH, DA_HEADS, SEQ // TQ, 2, 128, TQ), BF16),
            jax.ShapeDtypeStruct((BATCH * SEQ, QK_WIDTH), BF16),
            jax.ShapeDtypeStruct((BATCH, DA_HEADS, nt, V_ROWS, TM_IN), BF16),
            jax.ShapeDtypeStruct((BATCH, 2, SEQ, FT_WIDTH), BF16),
        ],
        compiler_params=pltpu.CompilerParams(
            dimension_semantics=("arbitrary", "arbitrary"), vmem_limit_bytes=VMEM_LIMIT),
        name="in_proj",
    )(x2, g_mix, w_in, cs, w_ft)


def _attn_kernel(ch_ref, lq1_ref, lk1_ref, lq2_ref, lk2_ref, gs_ref,
                 qt_ref, qaux_ref, k_ref, kaux_ref, vt_ref, o_ref, corr_ref,
                 sa_ref, sb_ref, ma_ref, mb_ref, acca_ref, accb_ref):
    hd = pl.program_id(0)
    bufs = ((sa_ref, ma_ref, acca_ref), (sb_ref, mb_ref, accb_ref))
    nch = SEQ // CH
    nq = SEQ // TQ
    tiles_per_chunk = CH // TQ

    @pl.when(pl.program_id(1) == 0)
    def _build_corr():
        n2slope = -2.0 * ch_ref[hd]
        d = (jax.lax.broadcasted_iota(jnp.int32, (TQ, TQ), 0)
             - jax.lax.broadcasted_iota(jnp.int32, (TQ, TQ), 1))
        for i in range(CORR_ROWS // TQ):
            off = (i - (tiles_per_chunk - 1)) * TQ
            corr_ref[i * TQ:(i + 1) * TQ, :] = jnp.maximum(d + off, 0).astype(F32) * n2slope

    def rows_of(r):
        return slice(r * CH, (r + 1) * CH)

    def split(g):
        return g // nq, g % nq

    def scores_chunk(g, t, par):
        s_ref, m_ref, _ = bufs[par]
        bb, u = split(g)
        rd = u // tiles_per_chunk
        r = (rd + t) % nch
        rows = pl.ds(pl.multiple_of(r * CH, CH), CH)
        krows = pl.ds(pl.multiple_of(bb * SEQ + r * CH, CH), CH)
        lhs = jnp.concatenate([k_ref[krows, :], kaux_ref[0, rows, :]], axis=1)
        after = 0 if t == 0 else jnp.asarray(r > rd, jnp.int32)
        aux = qaux_ref[0, u, after]
        for c in range(2):
            s = _dot(lhs, jnp.concatenate([qt_ref[bb, 0, u, c], aux], axis=0))
            if t == 0:
                start = pl.multiple_of((tiles_per_chunk - 1 - u % tiles_per_chunk) * TQ, TQ)
                s = s + corr_ref[pl.ds(start, CH), :]
            s_ref[c, rows, :] = s
            m_ref[c] = jnp.maximum(m_ref[c], jnp.max(s, axis=0, keepdims=True))

    def softmax_chunk(g, r, par):
        s_ref, m_ref, acc_ref = bufs[par]
        bb, _ = split(g)
        for c in range(2):
            p = jnp.exp2(s_ref[c, rows_of(r), :] - m_ref[c]).astype(BF16)
            lane0 = (r * CH) % TM_IN
            vt = vt_ref[bb, 0, (r * CH) // TM_IN, :, lane0:lane0 + CH]
            acc_ref[c] += _dot(vt, p)

    def finalize(g, par):
        acc_ref = bufs[par][2]
        lam = (jnp.exp(jnp.sum(lq1_ref[...] * lk1_ref[...], axis=-1, keepdims=True))
               - jnp.exp(jnp.sum(lq2_ref[...] * lk2_ref[...], axis=-1, keepdims=True))
               + LAMBDA_INIT)
        a1 = acc_ref[0]
        a2 = acc_ref[1]
        o1 = a1[:DA_V_DIM] / a1[DA_V_DIM:DA_V_DIM + 1]
        o2 = a2[:DA_V_DIM] / a2[DA_V_DIM:DA_V_DIM + 1]
        o = o1 - lam * o2
        ms = jnp.mean(o * o, axis=0, keepdims=True)
        y = o * jax.lax.rsqrt(ms + SUBLN_EPS) * gs_ref[...] * (1.0 - LAMBDA_INIT)
        o_ref[pl.ds(pl.multiple_of(g * TQ, TQ), TQ), :] = y.T.astype(BF16)
        acc_ref[...] = jnp.zeros(acc_ref.shape, F32)

    def step(g, par, scores=True, softmax=True, epilogue=True):
        if epilogue:
            finalize(g - 2, par)
        if scores:
            bufs[par][1][...] = jnp.full((2, 1, TQ), -1e30, F32)
        for r in range(nch):
            if scores:
                scores_chunk(g, r, par)
            if softmax:
                softmax_chunk(g - 1, r, 1 - par)

    nt = ATT_BATCHES * nq
    acca_ref[...] = jnp.zeros(acca_ref.shape, F32)
    accb_ref[...] = jnp.zeros(accb_ref.shape, F32)
    step(0, 0, softmax=False, epilogue=False)
    step(1, 1, epilogue=False)

    def step_pair(j, carry):
        step(2 * j + 2, 0)
        step(2 * j + 3, 1)
        return carry

    jax.lax.fori_loop(0, nt // 2 - 1, step_pair, 0)
    step(nt, 0, scores=False)
    finalize(nt - 1, 1)


def _attention(ch, lq1, lk1, lq2, lk2, gs_col, qt, qaux, kz, kaux, vt):
    nq = SEQ // TQ
    nch = SEQ // CH
    vec = pl.BlockSpec((1, DA_HEAD_DIM), lambda h, b: (0, 0))
    nb = ATT_BATCHES
    return pl.pallas_call(
        _attn_kernel,
        grid=(DA_HEADS, BATCH // nb),
        in_specs=[
            pl.BlockSpec(memory_space=pltpu.SMEM),
            vec, vec, vec, vec,
            pl.BlockSpec((DA_V_DIM, 1), lambda h, b: (0, 0)),
            pl.BlockSpec((nb, 1, nq, 2, 128, TQ), lambda h, b: (b, h, 0, 0, 0, 0)),
            pl.BlockSpec((1, nq, 2, 128, TQ), lambda h, b: (h, 0, 0, 0, 0)),
            pl.BlockSpec((nb * SEQ, 128), lambda h, b: (b, h)),
            pl.BlockSpec((1, SEQ, 128), lambda h, b: (h, 0, 0)),
            pl.BlockSpec((nb, 1, SEQ // TM_IN, V_ROWS, TM_IN), lambda h, b: (b, h, 0, 0, 0)),
        ],
        out_specs=pl.BlockSpec((nb * SEQ, DA_V_DIM), lambda h, b: (b, h)),
        out_shape=jax.ShapeDtypeStruct((BATCH * SEQ, DA_WIDTH), BF16),
        scratch_shapes=[pltpu.VMEM((CORR_ROWS, TQ), F32),
                        pltpu.VMEM((2, SEQ, TQ), F32), pltpu.VMEM((2, SEQ, TQ), F32),
                        pltpu.VMEM((2, 1, TQ), F32), pltpu.VMEM((2, 1, TQ), F32),
                        pltpu.VMEM((2, V_ROWS, TQ), F32), pltpu.VMEM((2, V_ROWS, TQ), F32)],
        compiler_params=pltpu.CompilerParams(
            dimension_semantics=("arbitrary", "arbitrary"), vmem_limit_bytes=VMEM_LIMIT),
        name="diff_attention",
    )(ch, lq1, lk1, lq2, lk2, gs_col, qt, qaux, kz, kaux, vt)


def _seq_dft_kernel(x_ref, w1_ref, m2_ref, o_ref, xs_ref, a_ref, y_ref):
    nl = DFT_COLS // 128

    def slab(v, l):
        return v[:, l * 128:(l + 1) * 128]

    for n1 in range(64):
        for p in range(2):
            v = x_ref[0, p, n1 * 64:(n1 + 1) * 64, :].astype(F32)
            for l in range(nl):
                xs_ref[p * nl + l, n1 * P1:n1 * P1 + 64, :] = slab(v, l)

    def stage1(n2, carry):
        parts = [jnp.concatenate([xs_ref[p * nl + l, pl.ds(n2, 64, stride=P1), :]
                                  for l in range(nl)], axis=1) for p in range(2)]
        d = jnp.concatenate(parts, axis=0).astype(BF16)
        a = _dot(w1_ref[...], d)
        row0 = pl.multiple_of(n2 * P2, 8)
        for l in range(nl):
            a_ref[l, pl.ds(row0, 128), :] = slab(a, l)
        return carry

    jax.lax.fori_loop(0, 64, stage1, 0, unroll=8)

    def stage2(k1, carry):
        parts = [jnp.concatenate([a_ref[l, pl.ds(k1 + 64 * p, 64, stride=P2), :]
                                  for l in range(nl)], axis=1) for p in range(2)]
        d = jnp.concatenate(parts, axis=0).astype(BF16)
        y = _dot(m2_ref[k1], d)
        row0 = pl.multiple_of(k1 * P1, 8)
        for l in range(nl):
            y_ref[l, pl.ds(row0, 64), :] = slab(y, l)
        return carry

    jax.lax.fori_loop(0, 64, stage2, 0, unroll=8)

    def stage3(k2, carry):
        row0 = pl.multiple_of(k2 * 64, 64)
        for l in range(nl):
            o_ref[pl.ds(row0, 64), l * 128:(l + 1) * 128] = (
                y_ref[l, pl.ds(k2, 64, stride=P1), :].astype(BF16))
        return carry

    jax.lax.fori_loop(0, 64, stage3, 0, unroll=8)


def _seq_dft(xcs, w1, m2):
    nh = FT_WIDTH // DFT_COLS
    nl = DFT_COLS // 128
    return pl.pallas_call(
        _seq_dft_kernel,
        grid=(BATCH, nh),
        in_specs=[
            pl.BlockSpec((1, 2, SEQ, DFT_COLS), lambda b, j: (b, 0, 0, j)),
            pl.BlockSpec((128, 128), lambda b, j: (0, 0)),
            pl.BlockSpec((64, 64, 128), lambda b, j: (0, 0, 0)),
        ],
        out_specs=pl.BlockSpec((SEQ, DFT_COLS), lambda b, j: (b, j)),
        out_shape=jax.ShapeDtypeStruct((BATCH * SEQ, FT_WIDTH), BF16),
        scratch_shapes=[pltpu.VMEM((2 * nl, 64 * P1, 128), F32),
                        pltpu.VMEM((nl, 64 * P2, 128), F32),
                        pltpu.VMEM((nl, 64 * P1, 128), F32)],
        compiler_params=pltpu.CompilerParams(
            dimension_semantics=("arbitrary", "arbitrary"), vmem_limit_bytes=VMEM_LIMIT),
        name="seq_dft",
    )(xcs, w1, m2)


def _rms(x, g, eps):
    ms = jnp.mean(x * x, axis=-1, keepdims=True)
    return x * jax.lax.rsqrt(ms + eps) * g


def _ffn_kernel(xc_ref, xp_ref, xn_ref, dac_ref, dap_ref, dan_ref, ftc_ref, ftp_ref, ftn_ref,
                wo_ref, gf_ref, wup_ref, wcv_ref, bcv_ref, wdn_ref, gl_ref, o_ref, act_ref):
    i = pl.program_id(0)
    tiles_per_seq = SEQ // TM_FFN
    pos = i % tiles_per_seq
    xe = jnp.concatenate([xp_ref[...], xc_ref[...], xn_ref[...]], axis=0)
    da = jnp.concatenate([dap_ref[...], dac_ref[...], dan_ref[...]], axis=0)
    ft = jnp.concatenate([ftp_ref[...], ftc_ref[...], ftn_ref[...]], axis=0)
    x1 = xe + _dot(da, wo_ref[0:DA_WIDTH, :]) + _dot(ft, wo_ref[DA_WIDTH:, :])
    h2 = _rms(x1, gf_ref[...], EPS)
    r = jax.lax.broadcasted_iota(jnp.int32, (TM_FFN + 2 * HALO, 1), 0)
    valid = jnp.logical_and(jnp.logical_or(r >= HALO, pos > 0),
                            jnp.logical_or(r < TM_FFN + HALO, pos < tiles_per_seq - 1))
    h2 = jnp.where(valid, h2, 0.0).astype(BF16)
    n_ext = TM_FFN + 2 * HALO

    def conv(up, col0):
        w = wcv_ref[:, col0:col0 + FF_CHUNK]
        prev = pltpu.roll(up, 1, 0)[HALO:HALO + TM_FFN]
        nxt = pltpu.roll(up, n_ext - 1, 0)[HALO:HALO + TM_FFN]
        cur = up[HALO:HALO + TM_FFN]
        return (w[0:1] * prev + w[1:2] * cur + w[2:3] * nxt
                + bcv_ref[:, col0:col0 + FF_CHUNK])

    for j in range(D_FF // FF_CHUNK):
        c0 = j * FF_CHUNK
        gate = conv(_dot(h2, wup_ref[:, c0:c0 + FF_CHUNK]), c0)
        val = conv(_dot(h2, wup_ref[:, D_FF + c0:D_FF + c0 + FF_CHUNK]), D_FF + c0)
        act = gate / (1.0 + jnp.exp(-gate)) * val
        act_ref[:, c0:c0 + FF_CHUNK] = act.astype(BF16)

    y = x1[HALO:HALO + TM_FFN] + _dot(act_ref[...], wdn_ref[...])
    o_ref[...] = _rms(y, gl_ref[...], EPS)


def _ffn(x2, o_da, o_ft, w_out, g_ffn, w_up, w_conv, b_conv, w_down, g_final):
    n = BATCH * SEQ // TM_FFN
    hb = TM_FFN // HALO
    last = BATCH * SEQ // HALO - 1
    cur = lambda i: (i, 0)
    prv = lambda i: (jnp.maximum(i * hb - 1, 0), 0)
    nxt = lambda i: (jnp.minimum((i + 1) * hb, last), 0)
    const = lambda i: (0, 0)

    def trio(width):
        return [pl.BlockSpec((TM_FFN, width), cur), pl.BlockSpec((HALO, width), prv),
                pl.BlockSpec((HALO, width), nxt)]

    def resident(shape):
        return pl.BlockSpec(shape, const, pipeline_mode=pl.Buffered(1))

    return pl.pallas_call(
        _ffn_kernel,
        grid=(n,),
        in_specs=trio(D_MODEL) + trio(DA_WIDTH) + trio(FT_WIDTH) + [
            resident((D_MODEL, D_MODEL)),
            resident((1, D_MODEL)),
            resident((D_MODEL, 2 * D_FF)),
            resident((3, 2 * D_FF)),
            resident((1, 2 * D_FF)),
            resident((D_FF, D_MODEL)),
            resident((1, D_MODEL)),
        ],
        out_specs=pl.BlockSpec((TM_FFN, D_MODEL), cur),
        out_shape=jax.ShapeDtypeStruct((BATCH * SEQ, D_MODEL), F32),
        scratch_shapes=[pltpu.VMEM((TM_FFN, D_FF), BF16)],
        compiler_params=pltpu.CompilerParams(
            dimension_semantics=("arbitrary",), vmem_limit_bytes=VMEM_LIMIT),
        name="out_proj_ffn",
    )(x2, x2, x2, o_da, o_da, o_da, o_ft, o_ft, o_ft,
      w_out, g_ffn, w_up, w_conv, b_conv, w_down, g_final)


@functools.lru_cache(maxsize=None)
def _dft_tables():
    i64 = np.arange(64, dtype=np.int64)
    a1 = 2.0 * np.pi * ((i64[:, None] * i64[None, :]) % 64) / 64.0
    c1, s1 = np.cos(a1), np.sin(a1)
    w1 = np.block([[c1, -s1], [-s1, -c1]])
    kk = i64[:, None, None] + 64 * i64[None, :, None]
    a2 = 2.0 * np.pi * ((kk * i64[None, None, :]) % SEQ) / SEQ
    norm = 1.0 / math.sqrt(SEQ)
    m2 = np.concatenate([np.cos(a2) * norm, np.sin(a2) * norm], axis=2)
    c = np.arange(FT_GROUP_DIM, dtype=np.int64)
    ac = 2.0 * np.pi * ((c[:, None] * c[None, :]) % FT_GROUP_DIM) / FT_GROUP_DIM
    cn = 1.0 / math.sqrt(FT_GROUP_DIM)
    cs = np.concatenate([np.cos(ac) * cn, np.sin(ac) * cn], axis=1)
    f = lambda a: np.asarray(a, dtype=np.float32)
    return f(w1), f(m2), f(cs)


def _slope_log2e(h):
    return np.float32(LOG2E * 2.0 ** (-8.0 * (h + 1) / DA_HEADS))


def _bf16_pieces(x, n):
    pieces, rem = [], np.asarray(x, dtype=np.float64)
    for _ in range(n):
        u = rem.astype(np.float32).view(np.uint32)
        u = (u + np.uint32(0x7FFF) + ((u >> np.uint32(16)) & np.uint32(1))) & np.uint32(0xFFFF0000)
        p = u.view(np.float32).astype(np.float64)
        pieces.append(p)
        rem = rem - p
    return pieces


@functools.lru_cache(maxsize=None)
def _alibi_tables():
    pos = np.arange(SEQ, dtype=np.float64)
    kaux = np.zeros((DA_HEADS, SEQ, 128), np.float64)
    qaux = np.zeros((DA_HEADS, 2, 128, SEQ), np.float64)
    for h in range(DA_HEADS):
        pieces = _bf16_pieces(np.float64(_slope_log2e(h)) * pos, N_SPLIT)
        for t, p in enumerate(pieces):
            kaux[h, :, t] = p
            qaux[h, 0, N_SPLIT + t, :] = -p
        kaux[h, :, N_SPLIT:2 * N_SPLIT] = 1.0
        qaux[h, 0, 0:N_SPLIT, :] = 1.0
        qaux[h, 1] = -qaux[h, 0]
    qaux = qaux.reshape(DA_HEADS, 2, 128, SEQ // TQ, TQ).transpose(0, 3, 1, 2, 4)
    return np.asarray(kaux, np.float32), np.ascontiguousarray(qaux, dtype=np.float32)


def kernel(x, g_mix, w_in, lambda_q1, lambda_k1, lambda_q2, lambda_k2, g_subln, w_ft, w_out,
           g_ffn, w_up, w_conv, b_conv, w_down, g_final):
    x2 = x.reshape(BATCH * SEQ, D_MODEL)
    w1, m2, cs = _dft_tables()
    qt, kz, vt, xcs = _in_proj(x2, g_mix[0][None, :], w_in[0].astype(BF16),
                               jnp.asarray(cs).astype(BF16), w_ft[0].astype(BF16))

    ch = jnp.asarray([_slope_log2e(h) for h in range(DA_HEADS)], F32)
    kaux, qaux = _alibi_tables()
    o_da = _attention(ch, lambda_q1, lambda_k1, lambda_q2, lambda_k2, g_subln[0][:, None],
                      qt, jnp.asarray(qaux).astype(BF16), kz, jnp.asarray(kaux).astype(BF16), vt)

    o_ft = _seq_dft(xcs, jnp.asarray(w1).astype(BF16), jnp.asarray(m2).astype(BF16))

    y = _ffn(x2, o_da, o_ft, w_out[0].astype(BF16), g_ffn[0][None, :], w_up[0].astype(BF16),
             w_conv[0], b_conv[0][None, :], w_down[0].astype(BF16), g_final[None, :])
    return y.reshape(BATCH, SEQ, D_MODEL)
```

```python
import functools
import math

import numpy as np
import jax
import jax.numpy as jnp
from jax.experimental import pallas as pl
from jax.experimental.pallas import tpu as pltpu

D_MODEL = 1024
BATCH = 8
SEQ = 4096
DA_HEADS = 4
DA_HEAD_DIM = 64
DA_V_DIM = 128
QK_WIDTH = 512
DA_WIDTH = 512
FT_GROUPS = 4
FT_GROUP_DIM = 128
FT_WIDTH = 512
IN_PROJ_WIDTH = 2048
D_FF = 2816
EPS = 1e-6
SUBLN_EPS = 1e-5
LAMBDA_INIT = 0.8 - 0.6 * math.exp(-0.3 * 0)
LOG2E = 1.4426950408889634

BF16 = jnp.bfloat16
F32 = jnp.float32

TM_IN = 1024
TQ = 256
CH = 512
ATT_BATCHES = 2
CORR_ROWS = CH + (CH // TQ - 1) * TQ
N_SPLIT = 4
V_ROWS = DA_V_DIM + 16
DFT_COLS = 256
P1 = 72
P2 = 136
TM_FFN = 1024
HALO = 16
FF_CHUNK = 256
VMEM_LIMIT = 56 * 1024 * 1024


def _dot(a, b):
    return jnp.dot(a, b, preferred_element_type=F32)


def _in_proj_kernel(x_ref, g_ref, w_ref, cs_ref, wft_ref, qt_ref, k_ref, vt_ref, xcs_ref,
                    wbf_ref, csw_ref):
    @pl.when(jnp.logical_and(pl.program_id(0) == 0, pl.program_id(1) == 0))
    def _prepare_weights():
        wbf_ref[...] = w_ref[...].astype(BF16)
        cs = cs_ref[...].astype(BF16)
        for g in range(FT_GROUPS):
            wg = wft_ref[g].astype(BF16)
            csw_ref[g] = jnp.concatenate([_dot(cs[:, :128], wg), _dot(cs[:, 128:], wg)],
                                         axis=1).astype(BF16)

    x = x_ref[...]
    ms = jnp.mean(x * x, axis=-1, keepdims=True)
    h = (x * jax.lax.rsqrt(ms + EPS) * g_ref[...]).astype(BF16)
    z = _dot(h, wbf_ref[...])
    row = jax.lax.broadcasted_iota(jnp.int32, (2 * DA_HEAD_DIM, TM_IN), 0)
    for hd in range(DA_HEADS):
        zq_t = (z[:, hd * 128:(hd + 1) * 128] * (DA_HEAD_DIM ** -0.5 * LOG2E)).T
        q1 = jnp.where(row < DA_HEAD_DIM, zq_t, 0.0).astype(BF16)
        q2 = jnp.where(row >= DA_HEAD_DIM, zq_t, 0.0).astype(BF16)
        for j in range(TM_IN // TQ):
            qt_ref[0, hd, j, 0] = q1[:, j * TQ:(j + 1) * TQ]
            qt_ref[0, hd, j, 1] = q2[:, j * TQ:(j + 1) * TQ]
        zv_t = z[:, 2 * QK_WIDTH + hd * 128:2 * QK_WIDTH + (hd + 1) * 128].T
        vt_ref[0, hd, 0, 0:DA_V_DIM, :] = zv_t.astype(BF16)
        ones_row = jax.lax.broadcasted_iota(jnp.int32, (V_ROWS - DA_V_DIM, TM_IN), 0) == 0
        vt_ref[0, hd, 0, DA_V_DIM:V_ROWS, :] = jnp.where(ones_row, 1.0, 0.0).astype(BF16)
    k_ref[...] = z[:, QK_WIDTH:2 * QK_WIDTH].astype(BF16)
    u0 = 2 * QK_WIDTH + DA_WIDTH
    for g in range(FT_GROUPS):
        ug = z[:, u0 + g * 128:u0 + (g + 1) * 128].astype(BF16)
        xcs = _dot(ug, csw_ref[g])
        xcs_ref[0, 0, :, g * 128:(g + 1) * 128] = xcs[:, :128].astype(BF16)
        xcs_ref[0, 1, :, g * 128:(g + 1) * 128] = xcs[:, 128:].astype(BF16)


def _in_proj(x2, g_mix, w_in, cs, w_ft):
    nt = SEQ // TM_IN
    return pl.pallas_call(
        _in_proj_kernel,
        grid=(BATCH, nt),
        in_specs=[
            pl.BlockSpec((TM_IN, D_MODEL), lambda b, t: (b * nt + t, 0)),
            pl.BlockSpec((1, D_MODEL), lambda b, t: (0, 0)),
            pl.BlockSpec((D_MODEL, IN_PROJ_WIDTH), lambda b, t: (0, 0),
                         pipeline_mode=pl.Buffered(1)),
            pl.BlockSpec((FT_GROUP_DIM, 2 * FT_GROUP_DIM), lambda b, t: (0, 0)),
            pl.BlockSpec((FT_GROUPS, FT_GROUP_DIM, FT_GROUP_DIM), lambda b, t: (0, 0, 0)),
        ],
        out_specs=[
            pl.BlockSpec((1, DA_HEADS, TM_IN // TQ, 2, 128, TQ),
                         lambda b, t: (b, 0, t, 0, 0, 0)),
            pl.BlockSpec((TM_IN, QK_WIDTH), lambda b, t: (b * nt + t, 0)),
            pl.BlockSpec((1, DA_HEADS, 1, V_ROWS, TM_IN), lambda b, t: (b, 0, t, 0, 0)),
            pl.BlockSpec((1, 2, TM_IN, FT_WIDTH), lambda b, t: (b, 0, t, 0)),
        ],
        out_shape=[
            jax.ShapeDtypeStruct((BATCH, DA_HEADS, SEQ // TQ, 2, 128, TQ), BF16),
            jax.ShapeDtypeStruct((BATCH * SEQ, QK_WIDTH), BF16),
            jax.ShapeDtypeStruct((BATCH, DA_HEADS, nt, V_ROWS, TM_IN), BF16),
            jax.ShapeDtypeStruct((BATCH, 2, SEQ, FT_WIDTH), BF16),
        ],
        scratch_shapes=[pltpu.VMEM((D_MODEL, IN_PROJ_WIDTH), BF16),
                        pltpu.VMEM((FT_GROUPS, FT_GROUP_DIM, 2 * FT_GROUP_DIM), BF16)],
        compiler_params=pltpu.CompilerParams(
            dimension_semantics=("arbitrary", "arbitrary"), vmem_limit_bytes=VMEM_LIMIT),
        name="in_proj",
    )(x2, g_mix, w_in, cs, w_ft)


def _attn_kernel(ch_ref, lq1_ref, lk1_ref, lq2_ref, lk2_ref, gs_ref,
                 qt_ref, qaux_ref, k_ref, kaux_ref, vt_ref, o_ref, corr_ref,
                 sa_ref, sb_ref, ma_ref, mb_ref, acca_ref, accb_ref):
    hd = pl.program_id(0)
    bufs = ((sa_ref, ma_ref, acca_ref), (sb_ref, mb_ref, accb_ref))
    nch = SEQ // CH
    nq = SEQ // TQ
    tiles_per_chunk = CH // TQ

    @pl.when(pl.program_id(1) == 0)
    def _build_corr():
        n2slope = -2.0 * ch_ref[hd]
        d = (jax.lax.broadcasted_iota(jnp.int32, (TQ, TQ), 0)
             - jax.lax.broadcasted_iota(jnp.int32, (TQ, TQ), 1))
        for i in range(CORR_ROWS // TQ):
            off = (i - (tiles_per_chunk - 1)) * TQ
            corr_ref[i * TQ:(i + 1) * TQ, :] = jnp.maximum(d + off, 0).astype(F32) * n2slope

    def rows_of(r):
        return slice(r * CH, (r + 1) * CH)

    def split(g):
        return g // nq, g % nq

    def scores_chunk(g, t, par):
        s_ref, m_ref, _ = bufs[par]
        bb, u = split(g)
        rd = u // tiles_per_chunk
        r = (rd + t) % nch
        rows = pl.ds(pl.multiple_of(r * CH, CH), CH)
        krows = pl.ds(pl.multiple_of(bb * SEQ + r * CH, CH), CH)
        lhs = jnp.concatenate([k_ref[krows, :], kaux_ref[0, rows, :]], axis=1)
        after = 0 if t == 0 else jnp.asarray(r > rd, jnp.int32)
        aux = qaux_ref[0, u, after]
        for c in range(2):
            s = _dot(lhs, jnp.concatenate([qt_ref[bb, 0, u, c], aux], axis=0))
            if t == 0:
                start = pl.multiple_of((tiles_per_chunk - 1 - u % tiles_per_chunk) * TQ, TQ)
                s = s + corr_ref[pl.ds(start, CH), :]
            s_ref[c, rows, :] = s
            m_ref[c] = jnp.maximum(m_ref[c], jnp.max(s, axis=0, keepdims=True))

    def softmax_chunk(g, r, par):
        s_ref, m_ref, acc_ref = bufs[par]
        bb, _ = split(g)
        for c in range(2):
            p = jnp.exp2(s_ref[c, rows_of(r), :] - m_ref[c]).astype(BF16)
            lane0 = (r * CH) % TM_IN
            vt = vt_ref[bb, 0, (r * CH) // TM_IN, :, lane0:lane0 + CH]
            acc_ref[c] += _dot(vt, p)

    def finalize(g, par):
        acc_ref = bufs[par][2]
        lam = (jnp.exp(jnp.sum(lq1_ref[...] * lk1_ref[...], axis=-1, keepdims=True))
               - jnp.exp(jnp.sum(lq2_ref[...] * lk2_ref[...], axis=-1, keepdims=True))
               + LAMBDA_INIT)
        a1 = acc_ref[0]
        a2 = acc_ref[1]
        o1 = a1[:DA_V_DIM] / a1[DA_V_DIM:DA_V_DIM + 1]
        o2 = a2[:DA_V_DIM] / a2[DA_V_DIM:DA_V_DIM + 1]
        o = o1 - lam * o2
        ms = jnp.mean(o * o, axis=0, keepdims=True)
        y = o * jax.lax.rsqrt(ms + SUBLN_EPS) * gs_ref[...] * (1.0 - LAMBDA_INIT)
        o_ref[pl.ds(pl.multiple_of(g * TQ, TQ), TQ), :] = y.T.astype(BF16)
        acc_ref[...] = jnp.zeros(acc_ref.shape, F32)

    def step(g, par, scores=True, softmax=True, epilogue=True):
        if epilogue:
            finalize(g - 2, par)
        if scores:
            bufs[par][1][...] = jnp.full((2, 1, TQ), -1e30, F32)
        for r in range(nch):
            if scores:
                scores_chunk(g, r, par)
            if softmax:
                softmax_chunk(g - 1, r, 1 - par)

    nt = ATT_BATCHES * nq
    acca_ref[...] = jnp.zeros(acca_ref.shape, F32)
    accb_ref[...] = jnp.zeros(accb_ref.shape, F32)
    step(0, 0, softmax=False, epilogue=False)
    step(1, 1, epilogue=False)

    def step_pair(j, carry):
        step(2 * j + 2, 0)
        step(2 * j + 3, 1)
        return carry

    jax.lax.fori_loop(0, nt // 2 - 1, step_pair, 0)
    step(nt, 0, scores=False)
    finalize(nt - 1, 1)


def _attention(ch, lq1, lk1, lq2, lk2, gs_col, qt, qaux, kz, kaux, vt):
    nq = SEQ // TQ
    nch = SEQ // CH
    vec = pl.BlockSpec((1, DA_HEAD_DIM), lambda h, b: (0, 0))
    nb = ATT_BATCHES
    return pl.pallas_call(
        _attn_kernel,
        grid=(DA_HEADS, BATCH // nb),
        in_specs=[
            pl.BlockSpec(memory_space=pltpu.SMEM),
            vec, vec, vec, vec,
            pl.BlockSpec((DA_V_DIM, 1), lambda h, b: (0, 0)),
            pl.BlockSpec((nb, 1, nq, 2, 128, TQ), lambda h, b: (b, h, 0, 0, 0, 0)),
            pl.BlockSpec((1, nq, 2, 128, TQ), lambda h, b: (h, 0, 0, 0, 0)),
            pl.BlockSpec((nb * SEQ, 128), lambda h, b: (b, h)),
            pl.BlockSpec((1, SEQ, 128), lambda h, b: (h, 0, 0)),
            pl.BlockSpec((nb, 1, SEQ // TM_IN, V_ROWS, TM_IN), lambda h, b: (b, h, 0, 0, 0)),
        ],
        out_specs=pl.BlockSpec((nb * SEQ, DA_V_DIM), lambda h, b: (b, h)),
        out_shape=jax.ShapeDtypeStruct((BATCH * SEQ, DA_WIDTH), BF16),
        scratch_shapes=[pltpu.VMEM((CORR_ROWS, TQ), F32),
                        pltpu.VMEM((2, SEQ, TQ), F32), pltpu.VMEM((2, SEQ, TQ), F32),
                        pltpu.VMEM((2, 1, TQ), F32), pltpu.VMEM((2, 1, TQ), F32),
                        pltpu.VMEM((2, V_ROWS, TQ), F32), pltpu.VMEM((2, V_ROWS, TQ), F32)],
        compiler_params=pltpu.CompilerParams(
            dimension_semantics=("arbitrary", "arbitrary"), vmem_limit_bytes=VMEM_LIMIT),
        name="diff_attention",
    )(ch, lq1, lk1, lq2, lk2, gs_col, qt, qaux, kz, kaux, vt)


def _seq_dft_kernel(x_ref, w1f_ref, m2f_ref, wup_ref, wdn_ref, wout_ref,
                    o_ref, wup_o_ref, wdn_o_ref, wout_o_ref,
                    w1_ref, m2_ref, xs_ref, a_ref, y_ref):
    nl = DFT_COLS // 128

    @pl.when(jnp.logical_and(pl.program_id(0) == 0, pl.program_id(1) == 0))
    def _prepare_matrices():
        w1_ref[...] = w1f_ref[...].astype(BF16)
        m2_ref[...] = m2f_ref[...].astype(BF16)

    wup_o_ref[...] = wup_ref[...].astype(BF16)
    wdn_o_ref[...] = wdn_ref[...].astype(BF16)
    wout_o_ref[...] = wout_ref[...].astype(BF16)

    def slab(v, l):
        return v[:, l * 128:(l + 1) * 128]

    for n1 in range(64):
        for p in range(2):
            v = x_ref[0, p, n1 * 64:(n1 + 1) * 64, :].astype(F32)
            for l in range(nl):
                xs_ref[p * nl + l, n1 * P1:n1 * P1 + 64, :] = slab(v, l)

    def stage1(n2, carry):
        parts = [jnp.concatenate([xs_ref[p * nl + l, pl.ds(n2, 64, stride=P1), :]
                                  for l in range(nl)], axis=1) for p in range(2)]
        d = jnp.concatenate(parts, axis=0).astype(BF16)
        a = _dot(w1_ref[...], d)
        row0 = pl.multiple_of(n2 * P2, 8)
        for l in range(nl):
            a_ref[l, pl.ds(row0, 128), :] = slab(a, l)
        return carry

    jax.lax.fori_loop(0, 64, stage1, 0, unroll=8)

    def stage2(k1, carry):
        parts = [jnp.concatenate([a_ref[l, pl.ds(k1 + 64 * p, 64, stride=P2), :]
                                  for l in range(nl)], axis=1) for p in range(2)]
        d = jnp.concatenate(parts, axis=0).astype(BF16)
        y = _dot(m2_ref[k1], d)
        row0 = pl.multiple_of(k1 * P1, 8)
        for l in range(nl):
            y_ref[l, pl.ds(row0, 64), :] = slab(y, l)
        return carry

    jax.lax.fori_loop(0, 64, stage2, 0, unroll=8)

    def stage3(k2, carry):
        row0 = pl.multiple_of(k2 * 64, 64)
        for l in range(nl):
            o_ref[pl.ds(row0, 64), l * 128:(l + 1) * 128] = (
                y_ref[l, pl.ds(k2, 64, stride=P1), :].astype(BF16))
        return carry

    jax.lax.fori_loop(0, 64, stage3, 0, unroll=8)


def _seq_dft(xcs, w1, m2, w_up, w_down, w_out):
    nh = FT_WIDTH // DFT_COLS
    nl = DFT_COLS // 128
    steps = BATCH * nh
    slab = lambda b, j: (b * nh + j, 0)

    def row_slab(w):
        return pl.BlockSpec((w.shape[0] // steps, w.shape[1]), slab)

    weights = (w_up, w_down, w_out)
    return pl.pallas_call(
        _seq_dft_kernel,
        grid=(BATCH, nh),
        in_specs=[
            pl.BlockSpec((1, 2, SEQ, DFT_COLS), lambda b, j: (b, 0, 0, j)),
            pl.BlockSpec((128, 128), lambda b, j: (0, 0)),
            pl.BlockSpec((64, 64, 128), lambda b, j: (0, 0, 0), pipeline_mode=pl.Buffered(1)),
        ] + [row_slab(w) for w in weights],
        out_specs=[pl.BlockSpec((SEQ, DFT_COLS), lambda b, j: (b, j))]
        + [row_slab(w) for w in weights],
        out_shape=[jax.ShapeDtypeStruct((BATCH * SEQ, FT_WIDTH), BF16)]
        + [jax.ShapeDtypeStruct(w.shape, BF16) for w in weights],
        scratch_shapes=[pltpu.VMEM((128, 128), BF16), pltpu.VMEM((64, 64, 128), BF16),
                        pltpu.VMEM((2 * nl, 64 * P1, 128), F32),
                        pltpu.VMEM((nl, 64 * P2, 128), F32),
                        pltpu.VMEM((nl, 64 * P1, 128), F32)],
        compiler_params=pltpu.CompilerParams(
            dimension_semantics=("arbitrary", "arbitrary"), vmem_limit_bytes=VMEM_LIMIT),
        name="seq_dft",
    )(xcs, w1, m2, w_up, w_down, w_out)


def _rms(x, g, eps):
    ms = jnp.mean(x * x, axis=-1, keepdims=True)
    return x * jax.lax.rsqrt(ms + eps) * g


def _ffn_kernel(xc_ref, xp_ref, xn_ref, dac_ref, dap_ref, dan_ref, ftc_ref, ftp_ref, ftn_ref,
                wo_ref, gf_ref, wup_ref, wcv_ref, bcv_ref, wdn_ref, gl_ref, o_ref, act_ref):
    i = pl.program_id(0)
    tiles_per_seq = SEQ // TM_FFN
    pos = i % tiles_per_seq
    xe = jnp.concatenate([xp_ref[...], xc_ref[...], xn_ref[...]], axis=0)
    da = jnp.concatenate([dap_ref[...], dac_ref[...], dan_ref[...]], axis=0)
    ft = jnp.concatenate([ftp_ref[...], ftc_ref[...], ftn_ref[...]], axis=0)
    x1 = xe + _dot(da, wo_ref[0:DA_WIDTH, :]) + _dot(ft, wo_ref[DA_WIDTH:, :])
    h2 = _rms(x1, gf_ref[...], EPS)
    r = jax.lax.broadcasted_iota(jnp.int32, (TM_FFN + 2 * HALO, 1), 0)
    valid = jnp.logical_and(jnp.logical_or(r >= HALO, pos > 0),
                            jnp.logical_or(r < TM_FFN + HALO, pos < tiles_per_seq - 1))
    h2 = jnp.where(valid, h2, 0.0).astype(BF16)
    n_ext = TM_FFN + 2 * HALO

    def conv(up, col0):
        w = wcv_ref[:, col0:col0 + FF_CHUNK]
        prev = pltpu.roll(up, 1, 0)[HALO:HALO + TM_FFN]
        nxt = pltpu.roll(up, n_ext - 1, 0)[HALO:HALO + TM_FFN]
        cur = up[HALO:HALO + TM_FFN]
        return (w[0:1] * prev + w[1:2] * cur + w[2:3] * nxt
                + bcv_ref[:, col0:col0 + FF_CHUNK])

    for j in range(D_FF // FF_CHUNK):
        c0 = j * FF_CHUNK
        gate = conv(_dot(h2, wup_ref[:, c0:c0 + FF_CHUNK]), c0)
        val = conv(_dot(h2, wup_ref[:, D_FF + c0:D_FF + c0 + FF_CHUNK]), D_FF + c0)
        act = gate / (1.0 + jnp.exp(-gate)) * val
        act_ref[:, c0:c0 + FF_CHUNK] = act.astype(BF16)

    y = x1[HALO:HALO + TM_FFN] + _dot(act_ref[...], wdn_ref[...])
    o_ref[...] = _rms(y, gl_ref[...], EPS)


def _ffn(x2, o_da, o_ft, w_out, g_ffn, w_up, w_conv, b_conv, w_down, g_final):
    n = BATCH * SEQ // TM_FFN
    hb = TM_FFN // HALO
    last = BATCH * SEQ // HALO - 1
    cur = lambda i: (i, 0)
    prv = lambda i: (jnp.maximum(i * hb - 1, 0), 0)
    nxt = lambda i: (jnp.minimum((i + 1) * hb, last), 0)
    const = lambda i: (0, 0)

    def trio(width):
        return [pl.BlockSpec((TM_FFN, width), cur), pl.BlockSpec((HALO, width), prv),
                pl.BlockSpec((HALO, width), nxt)]

    def resident(shape):
        return pl.BlockSpec(shape, const, pipeline_mode=pl.Buffered(1))

    return pl.pallas_call(
        _ffn_kernel,
        grid=(n,),
        in_specs=trio(D_MODEL) + trio(DA_WIDTH) + trio(FT_WIDTH) + [
            resident((D_MODEL, D_MODEL)),
            resident((1, D_MODEL)),
            resident((D_MODEL, 2 * D_FF)),
            resident((3, 2 * D_FF)),
            resident((1, 2 * D_FF)),
            resident((D_FF, D_MODEL)),
            resident((1, D_MODEL)),
        ],
        out_specs=pl.BlockSpec((TM_FFN, D_MODEL), cur),
        out_shape=jax.ShapeDtypeStruct((BATCH * SEQ, D_MODEL), F32),
        scratch_shapes=[pltpu.VMEM((TM_FFN, D_FF), BF16)],
        compiler_params=pltpu.CompilerParams(
            dimension_semantics=("arbitrary",), vmem_limit_bytes=VMEM_LIMIT),
        name="out_proj_ffn",
    )(x2, x2, x2, o_da, o_da, o_da, o_ft, o_ft, o_ft,
      w_out, g_ffn, w_up, w_conv, b_conv, w_down, g_final)


@functools.lru_cache(maxsize=None)
def _dft_tables():
    i64 = np.arange(64, dtype=np.int64)
    a1 = 2.0 * np.pi * ((i64[:, None] * i64[None, :]) % 64) / 64.0
    c1, s1 = np.cos(a1), np.sin(a1)
    w1 = np.block([[c1, -s1], [-s1, -c1]])
    kk = i64[:, None, None] + 64 * i64[None, :, None]
    a2 = 2.0 * np.pi * ((kk * i64[None, None, :]) % SEQ) / SEQ
    norm = 1.0 / math.sqrt(SEQ)
    m2 = np.concatenate([np.cos(a2) * norm, np.sin(a2) * norm], axis=2)
    c = np.arange(FT_GROUP_DIM, dtype=np.int64)
    ac = 2.0 * np.pi * ((c[:, None] * c[None, :]) % FT_GROUP_DIM) / FT_GROUP_DIM
    cn = 1.0 / math.sqrt(FT_GROUP_DIM)
    cs = np.concatenate([np.cos(ac) * cn, np.sin(ac) * cn], axis=1)
    f = lambda a: np.asarray(a, dtype=np.float32)
    return f(w1), f(m2), f(cs)


def _slope_log2e(h):
    return np.float32(LOG2E * 2.0 ** (-8.0 * (h + 1) / DA_HEADS))


def _bf16_pieces(x, n):
    pieces, rem = [], np.asarray(x, dtype=np.float64)
    for _ in range(n):
        u = rem.astype(np.float32).view(np.uint32)
        u = (u + np.uint32(0x7FFF) + ((u >> np.uint32(16)) & np.uint32(1))) & np.uint32(0xFFFF0000)
        p = u.view(np.float32).astype(np.float64)
        pieces.append(p)
        rem = rem - p
    return pieces


@functools.lru_cache(maxsize=None)
def _alibi_tables():
    pos = np.arange(SEQ, dtype=np.float64)
    kaux = np.zeros((DA_HEADS, SEQ, 128), np.float64)
    qaux = np.zeros((DA_HEADS, 2, 128, SEQ), np.float64)
    for h in range(DA_HEADS):
        pieces = _bf16_pieces(np.float64(_slope_log2e(h)) * pos, N_SPLIT)
        for t, p in enumerate(pieces):
            kaux[h, :, t] = p
            qaux[h, 0, N_SPLIT + t, :] = -p
        kaux[h, :, N_SPLIT:2 * N_SPLIT] = 1.0
        qaux[h, 0, 0:N_SPLIT, :] = 1.0
        qaux[h, 1] = -qaux[h, 0]
    qaux = qaux.reshape(DA_HEADS, 2, 128, SEQ // TQ, TQ).transpose(0, 3, 1, 2, 4)
    return np.asarray(kaux, dtype=BF16), np.ascontiguousarray(qaux).astype(BF16)


def kernel(x, g_mix, w_in, lambda_q1, lambda_k1, lambda_q2, lambda_k2, g_subln, w_ft, w_out,
           g_ffn, w_up, w_conv, b_conv, w_down, g_final):
    x2 = x.reshape(BATCH * SEQ, D_MODEL)
    w1, m2, cs = _dft_tables()
    qt, kz, vt, xcs = _in_proj(x2, g_mix[0][None, :], w_in[0], jnp.asarray(cs), w_ft[0])

    ch = jnp.asarray([_slope_log2e(h) for h in range(DA_HEADS)], F32)
    kaux, qaux = _alibi_tables()
    o_da = _attention(ch, lambda_q1, lambda_k1, lambda_q2, lambda_k2, g_subln[0][:, None],
                      qt, jnp.asarray(qaux), kz, jnp.asarray(kaux), vt)

    o_ft, w_up_bf, w_down_bf, w_out_bf = _seq_dft(xcs, jnp.asarray(w1), jnp.asarray(m2),
                                                  w_up[0], w_down[0], w_out[0])

    y = _ffn(x2, o_da, o_ft, w_out_bf, g_ffn[0][None, :], w_up_bf,
             w_conv[0], b_conv[0][None, :], w_down_bf, g_final[None, :])
    return y.reshape(BATCH, SEQ, D_MODEL)
```

```python
import functools
import math

import numpy as np
import jax
import jax.numpy as jnp
from jax.experimental import pallas as pl
from jax.experimental.pallas import tpu as pltpu

D_MODEL = 1024
BATCH = 8
SEQ = 4096
DA_HEADS = 4
DA_HEAD_DIM = 64
DA_V_DIM = 128
QK_WIDTH = 512
DA_WIDTH = 512
FT_GROUPS = 4
FT_GROUP_DIM = 128
FT_WIDTH = 512
IN_PROJ_WIDTH = 2048
D_FF = 2816
EPS = 1e-6
SUBLN_EPS = 1e-5
LAMBDA_INIT = 0.8 - 0.6 * math.exp(-0.3 * 0)
LOG2E = 1.4426950408889634

BF16 = jnp.bfloat16
F32 = jnp.float32

TM_IN = 1024
TQ = 256
CH = 512
ATT_BATCHES = 2
SKEW = 1
CORR_ROWS = CH + (CH // TQ - 1) * TQ
N_SPLIT = 4
V_ROWS = DA_V_DIM + 16
DFT_COLS = 256
P1 = 72
P2 = 136
TM_FFN = 1024
HALO = 16
FF_CHUNK = 256
VMEM_LIMIT = 56 * 1024 * 1024


def _dot(a, b):
    return jnp.dot(a, b, preferred_element_type=F32)


def _in_proj_kernel(x_ref, g_ref, w_ref, cs_ref, wft_ref, qt_ref, k_ref, vt_ref, xcs_ref,
                    wbf_ref, csw_ref):
    @pl.when(jnp.logical_and(pl.program_id(0) == 0, pl.program_id(1) == 0))
    def _prepare_weights():
        wbf_ref[...] = w_ref[...].astype(BF16)
        cs = cs_ref[...].astype(BF16)
        for g in range(FT_GROUPS):
            wg = wft_ref[g].astype(BF16)
            csw_ref[g] = jnp.concatenate([_dot(cs[:, :128], wg), _dot(cs[:, 128:], wg)],
                                         axis=1).astype(BF16)

    x = x_ref[...]
    ms = jnp.mean(x * x, axis=-1, keepdims=True)
    h = (x * jax.lax.rsqrt(ms + EPS) * g_ref[...]).astype(BF16)
    z = _dot(h, wbf_ref[...])
    row = jax.lax.broadcasted_iota(jnp.int32, (2 * DA_HEAD_DIM, TM_IN), 0)
    for hd in range(DA_HEADS):
        zq_t = (z[:, hd * 128:(hd + 1) * 128] * (DA_HEAD_DIM ** -0.5 * LOG2E)).T
        q1 = jnp.where(row < DA_HEAD_DIM, zq_t, 0.0).astype(BF16)
        q2 = jnp.where(row >= DA_HEAD_DIM, zq_t, 0.0).astype(BF16)
        for j in range(TM_IN // TQ):
            qt_ref[0, hd, j, 0] = q1[:, j * TQ:(j + 1) * TQ]
            qt_ref[0, hd, j, 1] = q2[:, j * TQ:(j + 1) * TQ]
        zv_t = z[:, 2 * QK_WIDTH + hd * 128:2 * QK_WIDTH + (hd + 1) * 128].T
        vt_ref[0, hd, 0, 0:DA_V_DIM, :] = zv_t.astype(BF16)
        ones_row = jax.lax.broadcasted_iota(jnp.int32, (V_ROWS - DA_V_DIM, TM_IN), 0) == 0
        vt_ref[0, hd, 0, DA_V_DIM:V_ROWS, :] = jnp.where(ones_row, 1.0, 0.0).astype(BF16)
    k_ref[...] = z[:, QK_WIDTH:2 * QK_WIDTH].astype(BF16)
    u0 = 2 * QK_WIDTH + DA_WIDTH
    for g in range(FT_GROUPS):
        ug = z[:, u0 + g * 128:u0 + (g + 1) * 128].astype(BF16)
        xcs = _dot(ug, csw_ref[g])
        xcs_ref[0, 0, :, g * 128:(g + 1) * 128] = xcs[:, :128].astype(BF16)
        xcs_ref[0, 1, :, g * 128:(g + 1) * 128] = xcs[:, 128:].astype(BF16)


def _in_proj(x2, g_mix, w_in, cs, w_ft):
    nt = SEQ // TM_IN
    return pl.pallas_call(
        _in_proj_kernel,
        grid=(BATCH, nt),
        in_specs=[
            pl.BlockSpec((TM_IN, D_MODEL), lambda b, t: (b * nt + t, 0)),
            pl.BlockSpec((1, D_MODEL), lambda b, t: (0, 0)),
            pl.BlockSpec((D_MODEL, IN_PROJ_WIDTH), lambda b, t: (0, 0),
                         pipeline_mode=pl.Buffered(1)),
            pl.BlockSpec((FT_GROUP_DIM, 2 * FT_GROUP_DIM), lambda b, t: (0, 0)),
            pl.BlockSpec((FT_GROUPS, FT_GROUP_DIM, FT_GROUP_DIM), lambda b, t: (0, 0, 0)),
        ],
        out_specs=[
            pl.BlockSpec((1, DA_HEADS, TM_IN // TQ, 2, 128, TQ),
                         lambda b, t: (b, 0, t, 0, 0, 0)),
            pl.BlockSpec((TM_IN, QK_WIDTH), lambda b, t: (b * nt + t, 0)),
            pl.BlockSpec((1, DA_HEADS, 1, V_ROWS, TM_IN), lambda b, t: (b, 0, t, 0, 0)),
            pl.BlockSpec((1, 2, TM_IN, FT_WIDTH), lambda b, t: (b, 0, t, 0)),
        ],
        out_shape=[
            jax.ShapeDtypeStruct((BATCH, DA_HEADS, SEQ // TQ, 2, 128, TQ), BF16),
            jax.ShapeDtypeStruct((BATCH * SEQ, QK_WIDTH), BF16),
            jax.ShapeDtypeStruct((BATCH, DA_HEADS, nt, V_ROWS, TM_IN), BF16),
            jax.ShapeDtypeStruct((BATCH, 2, SEQ, FT_WIDTH), BF16),
        ],
        scratch_shapes=[pltpu.VMEM((D_MODEL, IN_PROJ_WIDTH), BF16),
                        pltpu.VMEM((FT_GROUPS, FT_GROUP_DIM, 2 * FT_GROUP_DIM), BF16)],
        compiler_params=pltpu.CompilerParams(
            dimension_semantics=("arbitrary", "arbitrary"), vmem_limit_bytes=VMEM_LIMIT),
        name="in_proj",
    )(x2, g_mix, w_in, cs, w_ft)


def _attn_kernel(ch_ref, lq1_ref, lk1_ref, lq2_ref, lk2_ref, gs_ref,
                 qt_ref, qaux_ref, k_ref, kaux_ref, vt_ref, o_ref, corr_ref,
                 sa_ref, sb_ref, ma_ref, mb_ref, acca_ref, accb_ref):
    hd = pl.program_id(0)
    bufs = ((sa_ref, ma_ref, acca_ref), (sb_ref, mb_ref, accb_ref))
    nch = SEQ // CH
    nq = SEQ // TQ
    tiles_per_chunk = CH // TQ

    @pl.when(pl.program_id(1) == 0)
    def _build_corr():
        n2slope = -2.0 * ch_ref[hd]
        d = (jax.lax.broadcasted_iota(jnp.int32, (TQ, TQ), 0)
             - jax.lax.broadcasted_iota(jnp.int32, (TQ, TQ), 1))
        for i in range(CORR_ROWS // TQ):
            off = (i - (tiles_per_chunk - 1)) * TQ
            corr_ref[i * TQ:(i + 1) * TQ, :] = jnp.maximum(d + off, 0).astype(F32) * n2slope

    def rows_of(r):
        return slice(r * CH, (r + 1) * CH)

    def split(g):
        return g // nq, g % nq

    def scores_chunk(g, t, par, c):
        s_ref, m_ref, _ = bufs[par]
        bb, u = split(g)
        rd = u // tiles_per_chunk
        r = (rd + t) % nch
        rows = pl.ds(pl.multiple_of(r * CH, CH), CH)
        krows = pl.ds(pl.multiple_of(bb * SEQ + r * CH, CH), CH)
        lhs = jnp.concatenate([k_ref[krows, :], kaux_ref[0, rows, :]], axis=1)
        after = 0 if t == 0 else jnp.asarray(r > rd, jnp.int32)
        aux = qaux_ref[0, u, after]
        s = _dot(lhs, jnp.concatenate([qt_ref[bb, 0, u, c], aux], axis=0))
        if t == 0:
            start = pl.multiple_of((tiles_per_chunk - 1 - u % tiles_per_chunk) * TQ, TQ)
            s = s + corr_ref[pl.ds(start, CH), :]
        s_ref[c, rows, :] = s
        m_ref[c] = jnp.maximum(m_ref[c], jnp.max(s, axis=0, keepdims=True))

    def softmax_chunk(g, r, par, c):
        s_ref, m_ref, acc_ref = bufs[par]
        bb, _ = split(g)
        p = jnp.exp2(s_ref[c, rows_of(r), :] - m_ref[c]).astype(BF16)
        lane0 = (r * CH) % TM_IN
        vt = vt_ref[bb, 0, (r * CH) // TM_IN, :, lane0:lane0 + CH]
        acc_ref[c] += _dot(vt, p)

    def finalize(g, par):
        acc_ref = bufs[par][2]
        lam = (jnp.exp(jnp.sum(lq1_ref[...] * lk1_ref[...], axis=-1, keepdims=True))
               - jnp.exp(jnp.sum(lq2_ref[...] * lk2_ref[...], axis=-1, keepdims=True))
               + LAMBDA_INIT)
        a1 = acc_ref[0]
        a2 = acc_ref[1]
        o1 = a1[:DA_V_DIM] / a1[DA_V_DIM:DA_V_DIM + 1]
        o2 = a2[:DA_V_DIM] / a2[DA_V_DIM:DA_V_DIM + 1]
        o = o1 - lam * o2
        ms = jnp.mean(o * o, axis=0, keepdims=True)
        y = o * jax.lax.rsqrt(ms + SUBLN_EPS) * gs_ref[...] * (1.0 - LAMBDA_INIT)
        o_ref[pl.ds(pl.multiple_of(g * TQ, TQ), TQ), :] = y.T.astype(BF16)
        acc_ref[...] = jnp.zeros(acc_ref.shape, F32)

    def step(g, par, scores=True, softmax=True, epilogue=True):
        if epilogue:
            finalize(g - 2, par)
        if scores:
            bufs[par][1][...] = jnp.full((2, 1, TQ), -1e30, F32)
        for r in range(nch + SKEW):
            if scores and r < nch:
                for c in range(2):
                    scores_chunk(g, r, par, c)
            if softmax and r >= SKEW:
                for c in range(2):
                    softmax_chunk(g - 1, r - SKEW, 1 - par, c)

    nt = ATT_BATCHES * nq
    acca_ref[...] = jnp.zeros(acca_ref.shape, F32)
    accb_ref[...] = jnp.zeros(accb_ref.shape, F32)
    step(0, 0, softmax=False, epilogue=False)
    step(1, 1, epilogue=False)

    def step_pair(j, carry):
        step(2 * j + 2, 0)
        step(2 * j + 3, 1)
        return carry

    jax.lax.fori_loop(0, nt // 2 - 1, step_pair, 0)
    step(nt, 0, scores=False)
    finalize(nt - 1, 1)


def _attention(ch, lq1, lk1, lq2, lk2, gs_col, qt, qaux, kz, kaux, vt):
    nq = SEQ // TQ
    nch = SEQ // CH
    vec = pl.BlockSpec((1, DA_HEAD_DIM), lambda h, b: (0, 0))
    nb = ATT_BATCHES
    return pl.pallas_call(
        _attn_kernel,
        grid=(DA_HEADS, BATCH // nb),
        in_specs=[
            pl.BlockSpec(memory_space=pltpu.SMEM),
            vec, vec, vec, vec,
            pl.BlockSpec((DA_V_DIM, 1), lambda h, b: (0, 0)),
            pl.BlockSpec((nb, 1, nq, 2, 128, TQ), lambda h, b: (b, h, 0, 0, 0, 0)),
            pl.BlockSpec((1, nq, 2, 128, TQ), lambda h, b: (h, 0, 0, 0, 0)),
            pl.BlockSpec((nb * SEQ, 128), lambda h, b: (b, h)),
            pl.BlockSpec((1, SEQ, 128), lambda h, b: (h, 0, 0)),
            pl.BlockSpec((nb, 1, SEQ // TM_IN, V_ROWS, TM_IN), lambda h, b: (b, h, 0, 0, 0)),
        ],
        out_specs=pl.BlockSpec((nb * SEQ, DA_V_DIM), lambda h, b: (b, h)),
        out_shape=jax.ShapeDtypeStruct((BATCH * SEQ, DA_WIDTH), BF16),
        scratch_shapes=[pltpu.VMEM((CORR_ROWS, TQ), F32),
                        pltpu.VMEM((2, SEQ, TQ), F32), pltpu.VMEM((2, SEQ, TQ), F32),
                        pltpu.VMEM((2, 1, TQ), F32), pltpu.VMEM((2, 1, TQ), F32),
                        pltpu.VMEM((2, V_ROWS, TQ), F32), pltpu.VMEM((2, V_ROWS, TQ), F32)],
        compiler_params=pltpu.CompilerParams(
            dimension_semantics=("arbitrary", "arbitrary"), vmem_limit_bytes=VMEM_LIMIT),
        name="diff_attention",
    )(ch, lq1, lk1, lq2, lk2, gs_col, qt, qaux, kz, kaux, vt)


def _seq_dft_kernel(x_ref, w1f_ref, m2f_ref, wup_ref, wdn_ref, wout_ref,
                    o_ref, wup_o_ref, wdn_o_ref, wout_o_ref,
                    w1_ref, m2_ref, xs_ref, a_ref, y_ref):
    nl = DFT_COLS // 128

    @pl.when(jnp.logical_and(pl.program_id(0) == 0, pl.program_id(1) == 0))
    def _prepare_matrices():
        w1_ref[...] = w1f_ref[...].astype(BF16)
        m2_ref[...] = m2f_ref[...].astype(BF16)

    wup_o_ref[...] = wup_ref[...].astype(BF16)
    wdn_o_ref[...] = wdn_ref[...].astype(BF16)
    wout_o_ref[...] = wout_ref[...].astype(BF16)

    def slab(v, l):
        return v[:, l * 128:(l + 1) * 128]

    for n1 in range(64):
        for p in range(2):
            v = x_ref[0, p, n1 * 64:(n1 + 1) * 64, :].astype(F32)
            for l in range(nl):
                xs_ref[p * nl + l, n1 * P1:n1 * P1 + 64, :] = slab(v, l)

    def stage1(n2, carry):
        parts = [jnp.concatenate([xs_ref[p * nl + l, pl.ds(n2, 64, stride=P1), :]
                                  for l in range(nl)], axis=1) for p in range(2)]
        d = jnp.concatenate(parts, axis=0).astype(BF16)
        a = _dot(w1_ref[...], d)
        row0 = pl.multiple_of(n2 * P2, 8)
        for l in range(nl):
            a_ref[l, pl.ds(row0, 128), :] = slab(a, l)
        return carry

    jax.lax.fori_loop(0, 64, stage1, 0, unroll=8)

    def stage2(k1, carry):
        parts = [jnp.concatenate([a_ref[l, pl.ds(k1 + 64 * p, 64, stride=P2), :]
                                  for l in range(nl)], axis=1) for p in range(2)]
        d = jnp.concatenate(parts, axis=0).astype(BF16)
        y = _dot(m2_ref[k1], d)
        row0 = pl.multiple_of(k1 * P1, 8)
        for l in range(nl):
            y_ref[l, pl.ds(row0, 64), :] = slab(y, l)
        return carry

    jax.lax.fori_loop(0, 64, stage2, 0, unroll=8)

    def stage3(k2, carry):
        row0 = pl.multiple_of(k2 * 64, 64)
        for l in range(nl):
            o_ref[pl.ds(row0, 64), l * 128:(l + 1) * 128] = (
                y_ref[l, pl.ds(k2, 64, stride=P1), :].astype(BF16))
        return carry

    jax.lax.fori_loop(0, 64, stage3, 0, unroll=8)


def _seq_dft(xcs, w1, m2, w_up, w_down, w_out):
    nh = FT_WIDTH // DFT_COLS
    nl = DFT_COLS // 128
    steps = BATCH * nh
    slab = lambda b, j: (b * nh + j, 0)

    def row_slab(w):
        return pl.BlockSpec((w.shape[0] // steps, w.shape[1]), slab)

    weights = (w_up, w_down, w_out)
    return pl.pallas_call(
        _seq_dft_kernel,
        grid=(BATCH, nh),
        in_specs=[
            pl.BlockSpec((1, 2, SEQ, DFT_COLS), lambda b, j: (b, 0, 0, j)),
            pl.BlockSpec((128, 128), lambda b, j: (0, 0)),
            pl.BlockSpec((64, 64, 128), lambda b, j: (0, 0, 0), pipeline_mode=pl.Buffered(1)),
        ] + [row_slab(w) for w in weights],
        out_specs=[pl.BlockSpec((SEQ, DFT_COLS), lambda b, j: (b, j))]
        + [row_slab(w) for w in weights],
        out_shape=[jax.ShapeDtypeStruct((BATCH * SEQ, FT_WIDTH), BF16)]
        + [jax.ShapeDtypeStruct(w.shape, BF16) for w in weights],
        scratch_shapes=[pltpu.VMEM((128, 128), BF16), pltpu.VMEM((64, 64, 128), BF16),
                        pltpu.VMEM((2 * nl, 64 * P1, 128), F32),
                        pltpu.VMEM((nl, 64 * P2, 128), F32),
                        pltpu.VMEM((nl, 64 * P1, 128), F32)],
        compiler_params=pltpu.CompilerParams(
            dimension_semantics=("arbitrary", "arbitrary"), vmem_limit_bytes=VMEM_LIMIT),
        name="seq_dft",
    )(xcs, w1, m2, w_up, w_down, w_out)


def _rms(x, g, eps):
    ms = jnp.mean(x * x, axis=-1, keepdims=True)
    return x * jax.lax.rsqrt(ms + eps) * g


def _ffn_kernel(xc_ref, xp_ref, xn_ref, dac_ref, dap_ref, dan_ref, ftc_ref, ftp_ref, ftn_ref,
                wo_ref, gf_ref, wup_ref, wcv_ref, bcv_ref, wdn_ref, gl_ref, o_ref, act_ref):
    i = pl.program_id(0)
    tiles_per_seq = SEQ // TM_FFN
    pos = i % tiles_per_seq
    xe = jnp.concatenate([xp_ref[...], xc_ref[...], xn_ref[...]], axis=0)
    da = jnp.concatenate([dap_ref[...], dac_ref[...], dan_ref[...]], axis=0)
    ft = jnp.concatenate([ftp_ref[...], ftc_ref[...], ftn_ref[...]], axis=0)
    x1 = xe + _dot(da, wo_ref[0:DA_WIDTH, :]) + _dot(ft, wo_ref[DA_WIDTH:, :])
    h2 = _rms(x1, gf_ref[...], EPS)
    r = jax.lax.broadcasted_iota(jnp.int32, (TM_FFN + 2 * HALO, 1), 0)
    valid = jnp.logical_and(jnp.logical_or(r >= HALO, pos > 0),
                            jnp.logical_or(r < TM_FFN + HALO, pos < tiles_per_seq - 1))
    h2 = jnp.where(valid, h2, 0.0).astype(BF16)
    n_ext = TM_FFN + 2 * HALO

    def conv(up, col0):
        w = wcv_ref[:, col0:col0 + FF_CHUNK]
        prev = pltpu.roll(up, 1, 0)[HALO:HALO + TM_FFN]
        nxt = pltpu.roll(up, n_ext - 1, 0)[HALO:HALO + TM_FFN]
        cur = up[HALO:HALO + TM_FFN]
        return (w[0:1] * prev + w[1:2] * cur + w[2:3] * nxt
                + bcv_ref[:, col0:col0 + FF_CHUNK])

    for j in range(D_FF // FF_CHUNK):
        c0 = j * FF_CHUNK
        gate = conv(_dot(h2, wup_ref[:, c0:c0 + FF_CHUNK]), c0)
        val = conv(_dot(h2, wup_ref[:, D_FF + c0:D_FF + c0 + FF_CHUNK]), D_FF + c0)
        act = gate / (1.0 + jnp.exp(-gate)) * val
        act_ref[:, c0:c0 + FF_CHUNK] = act.astype(BF16)

    y = x1[HALO:HALO + TM_FFN] + _dot(act_ref[...], wdn_ref[...])
    o_ref[...] = _rms(y, gl_ref[...], EPS)


def _ffn(x2, o_da, o_ft, w_out, g_ffn, w_up, w_conv, b_conv, w_down, g_final):
    n = BATCH * SEQ // TM_FFN
    hb = TM_FFN // HALO
    last = BATCH * SEQ // HALO - 1
    cur = lambda i: (i, 0)
    prv = lambda i: (jnp.maximum(i * hb - 1, 0), 0)
    nxt = lambda i: (jnp.minimum((i + 1) * hb, last), 0)
    const = lambda i: (0, 0)

    def trio(width):
        return [pl.BlockSpec((TM_FFN, width), cur), pl.BlockSpec((HALO, width), prv),
                pl.BlockSpec((HALO, width), nxt)]

    def resident(shape):
        return pl.BlockSpec(shape, const, pipeline_mode=pl.Buffered(1))

    return pl.pallas_call(
        _ffn_kernel,
        grid=(n,),
        in_specs=trio(D_MODEL) + trio(DA_WIDTH) + trio(FT_WIDTH) + [
            resident((D_MODEL, D_MODEL)),
            resident((1, D_MODEL)),
            resident((D_MODEL, 2 * D_FF)),
            resident((3, 2 * D_FF)),
            resident((1, 2 * D_FF)),
            resident((D_FF, D_MODEL)),
            resident((1, D_MODEL)),
        ],
        out_specs=pl.BlockSpec((TM_FFN, D_MODEL), cur),
        out_shape=jax.ShapeDtypeStruct((BATCH * SEQ, D_MODEL), F32),
        scratch_shapes=[pltpu.VMEM((TM_FFN, D_FF), BF16)],
        compiler_params=pltpu.CompilerParams(
            dimension_semantics=("arbitrary",), vmem_limit_bytes=VMEM_LIMIT),
        name="out_proj_ffn",
    )(x2, x2, x2, o_da, o_da, o_da, o_ft, o_ft, o_ft,
      w_out, g_ffn, w_up, w_conv, b_conv, w_down, g_final)


@functools.lru_cache(maxsize=None)
def _dft_tables():
    i64 = np.arange(64, dtype=np.int64)
    a1 = 2.0 * np.pi * ((i64[:, None] * i64[None, :]) % 64) / 64.0
    c1, s1 = np.cos(a1), np.sin(a1)
    w1 = np.block([[c1, -s1], [-s1, -c1]])
    kk = i64[:, None, None] + 64 * i64[None, :, None]
    a2 = 2.0 * np.pi * ((kk * i64[None, None, :]) % SEQ) / SEQ
    norm = 1.0 / math.sqrt(SEQ)
    m2 = np.concatenate([np.cos(a2) * norm, np.sin(a2) * norm], axis=2)
    c = np.arange(FT_GROUP_DIM, dtype=np.int64)
    ac = 2.0 * np.pi * ((c[:, None] * c[None, :]) % FT_GROUP_DIM) / FT_GROUP_DIM
    cn = 1.0 / math.sqrt(FT_GROUP_DIM)
    cs = np.concatenate([np.cos(ac) * cn, np.sin(ac) * cn], axis=1)
    f = lambda a: np.asarray(a, dtype=np.float32)
    return f(w1), f(m2), f(cs)


def _slope_log2e(h):
    return np.float32(LOG2E * 2.0 ** (-8.0 * (h + 1) / DA_HEADS))


def _bf16_pieces(x, n):
    pieces, rem = [], np.asarray(x, dtype=np.float64)
    for _ in range(n):
        u = rem.astype(np.float32).view(np.uint32)
        u = (u + np.uint32(0x7FFF) + ((u >> np.uint32(16)) & np.uint32(1))) & np.uint32(0xFFFF0000)
        p = u.view(np.float32).astype(np.float64)
        pieces.append(p)
        rem = rem - p
    return pieces


@functools.lru_cache(maxsize=None)
def _alibi_tables():
    pos = np.arange(SEQ, dtype=np.float64)
    kaux = np.zeros((DA_HEADS, SEQ, 128), np.float64)
    qaux = np.zeros((DA_HEADS, 2, 128, SEQ), np.float64)
    for h in range(DA_HEADS):
        pieces = _bf16_pieces(np.float64(_slope_log2e(h)) * pos, N_SPLIT)
        for t, p in enumerate(pieces):
            kaux[h, :, t] = p
            qaux[h, 0, N_SPLIT + t, :] = -p
        kaux[h, :, N_SPLIT:2 * N_SPLIT] = 1.0
        qaux[h, 0, 0:N_SPLIT, :] = 1.0
        qaux[h, 1] = -qaux[h, 0]
    qaux = qaux.reshape(DA_HEADS, 2, 128, SEQ // TQ, TQ).transpose(0, 3, 1, 2, 4)
    return np.asarray(kaux, dtype=BF16), np.ascontiguousarray(qaux).astype(BF16)


def kernel(x, g_mix, w_in, lambda_q1, lambda_k1, lambda_q2, lambda_k2, g_subln, w_ft, w_out,
           g_ffn, w_up, w_conv, b_conv, w_down, g_final):
    x2 = x.reshape(BATCH * SEQ, D_MODEL)
    w1, m2, cs = _dft_tables()
    qt, kz, vt, xcs = _in_proj(x2, g_mix[0][None, :], w_in[0], jnp.asarray(cs), w_ft[0])

    ch = jnp.asarray([_slope_log2e(h) for h in range(DA_HEADS)], F32)
    kaux, qaux = _alibi_tables()
    o_da = _attention(ch, lambda_q1, lambda_k1, lambda_q2, lambda_k2, g_subln[0][:, None],
                      qt, jnp.asarray(qaux), kz, jnp.asarray(kaux), vt)

    o_ft, w_up_bf, w_down_bf, w_out_bf = _seq_dft(xcs, jnp.asarray(w1), jnp.asarray(m2),
                                                  w_up[0], w_down[0], w_out[0])

    y = _ffn(x2, o_da, o_ft, w_out_bf, g_ffn[0][None, :], w_up_bf,
             w_conv[0], b_conv[0][None, :], w_down_bf, g_final[None, :])
    return y.reshape(BATCH, SEQ, D_MODEL)
```

```python
import functools
import math

import numpy as np
import jax
import jax.numpy as jnp
from jax.experimental import pallas as pl
from jax.experimental.pallas import tpu as pltpu

D_MODEL = 1024
BATCH = 8
SEQ = 4096
DA_HEADS = 4
DA_HEAD_DIM = 64
DA_V_DIM = 128
QK_WIDTH = 512
DA_WIDTH = 512
FT_GROUPS = 4
FT_GROUP_DIM = 128
FT_WIDTH = 512
IN_PROJ_WIDTH = 2048
D_FF = 2816
EPS = 1e-6
SUBLN_EPS = 1e-5
LAMBDA_INIT = 0.8 - 0.6 * math.exp(-0.3 * 0)
LOG2E = 1.4426950408889634

BF16 = jnp.bfloat16
F32 = jnp.float32

TM_IN = 1024
TQ = 256
CH = 512
ATT_BATCHES = 1
CORR_ROWS = CH + (CH // TQ - 1) * TQ
N_SPLIT = 4
V_ROWS = DA_V_DIM + 16
DFT_COLS = 256
P1 = 72
P2 = 136
TM_FFN = 1024
HALO = 16
FF_CHUNK = 256
VMEM_LIMIT = 56 * 1024 * 1024


def _dot(a, b):
    return jnp.dot(a, b, preferred_element_type=F32)


def _in_proj_kernel(x_ref, g_ref, w_ref, cs_ref, wft_ref, qt_ref, k_ref, vt_ref, xcs_ref,
                    wbf_ref, csw_ref):
    @pl.when(jnp.logical_and(pl.program_id(0) == 0, pl.program_id(1) == 0))
    def _prepare_weights():
        wbf_ref[...] = w_ref[...].astype(BF16)
        cs = cs_ref[...].astype(BF16)
        for g in range(FT_GROUPS):
            wg = wft_ref[g].astype(BF16)
            csw_ref[g] = jnp.concatenate([_dot(cs[:, :128], wg), _dot(cs[:, 128:], wg)],
                                         axis=1).astype(BF16)

    x = x_ref[...]
    ms = jnp.mean(x * x, axis=-1, keepdims=True)
    h = (x * jax.lax.rsqrt(ms + EPS) * g_ref[...]).astype(BF16)
    z = _dot(h, wbf_ref[...])
    row = jax.lax.broadcasted_iota(jnp.int32, (2 * DA_HEAD_DIM, TM_IN), 0)
    for hd in range(DA_HEADS):
        zq_t = (z[:, hd * 128:(hd + 1) * 128] * (DA_HEAD_DIM ** -0.5 * LOG2E)).T
        q1 = jnp.where(row < DA_HEAD_DIM, zq_t, 0.0).astype(BF16)
        q2 = jnp.where(row >= DA_HEAD_DIM, zq_t, 0.0).astype(BF16)
        for j in range(TM_IN // TQ):
            qt_ref[0, hd, j, 0] = q1[:, j * TQ:(j + 1) * TQ]
            qt_ref[0, hd, j, 1] = q2[:, j * TQ:(j + 1) * TQ]
        zv_t = z[:, 2 * QK_WIDTH + hd * 128:2 * QK_WIDTH + (hd + 1) * 128].T
        vt_ref[0, hd, 0, 0:DA_V_DIM, :] = zv_t.astype(BF16)
        ones_row = jax.lax.broadcasted_iota(jnp.int32, (V_ROWS - DA_V_DIM, TM_IN), 0) == 0
        vt_ref[0, hd, 0, DA_V_DIM:V_ROWS, :] = jnp.where(ones_row, 1.0, 0.0).astype(BF16)
    k_ref[...] = z[:, QK_WIDTH:2 * QK_WIDTH].astype(BF16)
    u0 = 2 * QK_WIDTH + DA_WIDTH
    for g in range(FT_GROUPS):
        ug = z[:, u0 + g * 128:u0 + (g + 1) * 128].astype(BF16)
        xcs = _dot(ug, csw_ref[g])
        xcs_ref[0, 0, :, g * 128:(g + 1) * 128] = xcs[:, :128].astype(BF16)
        xcs_ref[0, 1, :, g * 128:(g + 1) * 128] = xcs[:, 128:].astype(BF16)


def _in_proj(x2, g_mix, w_in, cs, w_ft):
    nt = SEQ // TM_IN
    return pl.pallas_call(
        _in_proj_kernel,
        grid=(BATCH, nt),
        in_specs=[
            pl.BlockSpec((TM_IN, D_MODEL), lambda b, t: (b * nt + t, 0)),
            pl.BlockSpec((1, D_MODEL), lambda b, t: (0, 0)),
            pl.BlockSpec((D_MODEL, IN_PROJ_WIDTH), lambda b, t: (0, 0),
                         pipeline_mode=pl.Buffered(1)),
            pl.BlockSpec((FT_GROUP_DIM, 2 * FT_GROUP_DIM), lambda b, t: (0, 0)),
            pl.BlockSpec((FT_GROUPS, FT_GROUP_DIM, FT_GROUP_DIM), lambda b, t: (0, 0, 0)),
        ],
        out_specs=[
            pl.BlockSpec((1, DA_HEADS, TM_IN // TQ, 2, 128, TQ),
                         lambda b, t: (b, 0, t, 0, 0, 0)),
            pl.BlockSpec((TM_IN, QK_WIDTH), lambda b, t: (b * nt + t, 0)),
            pl.BlockSpec((1, DA_HEADS, 1, V_ROWS, TM_IN), lambda b, t: (b, 0, t, 0, 0)),
            pl.BlockSpec((1, 2, TM_IN, FT_WIDTH), lambda b, t: (b, 0, t, 0)),
        ],
        out_shape=[
            jax.ShapeDtypeStruct((BATCH, DA_HEADS, SEQ // TQ, 2, 128, TQ), BF16),
            jax.ShapeDtypeStruct((BATCH * SEQ, QK_WIDTH), BF16),
            jax.ShapeDtypeStruct((BATCH, DA_HEADS, nt, V_ROWS, TM_IN), BF16),
            jax.ShapeDtypeStruct((BATCH, 2, SEQ, FT_WIDTH), BF16),
        ],
        scratch_shapes=[pltpu.VMEM((D_MODEL, IN_PROJ_WIDTH), BF16),
                        pltpu.VMEM((FT_GROUPS, FT_GROUP_DIM, 2 * FT_GROUP_DIM), BF16)],
        compiler_params=pltpu.CompilerParams(
            dimension_semantics=("arbitrary", "arbitrary"), vmem_limit_bytes=VMEM_LIMIT),
        name="in_proj",
    )(x2, g_mix, w_in, cs, w_ft)


def _attn_kernel(ch_ref, lq1_ref, lk1_ref, lq2_ref, lk2_ref, gs_ref,
                 qt_ref, qaux_ref, k_ref, kaux_ref, vt_ref, o_ref, corr_ref,
                 sa_ref, sb_ref, ma_ref, mb_ref, acca_ref, accb_ref):
    hd = pl.program_id(0)
    bufs = ((sa_ref, ma_ref, acca_ref), (sb_ref, mb_ref, accb_ref))
    nch = SEQ // CH
    nq = SEQ // TQ
    tiles_per_chunk = CH // TQ

    @pl.when(pl.program_id(1) == 0)
    def _build_corr():
        n2slope = -2.0 * ch_ref[hd]
        d = (jax.lax.broadcasted_iota(jnp.int32, (TQ, TQ), 0)
             - jax.lax.broadcasted_iota(jnp.int32, (TQ, TQ), 1))
        for i in range(CORR_ROWS // TQ):
            off = (i - (tiles_per_chunk - 1)) * TQ
            corr_ref[i * TQ:(i + 1) * TQ, :] = jnp.maximum(d + off, 0).astype(F32) * n2slope

    def rows_of(r):
        return slice(r * CH, (r + 1) * CH)

    def split(g):
        return g // nq, g % nq

    def scores_chunk(g, t, par):
        s_ref, m_ref, _ = bufs[par]
        bb, u = split(g)
        rd = u // tiles_per_chunk
        r = (rd + t) % nch
        rows = pl.ds(pl.multiple_of(r * CH, CH), CH)
        krows = pl.ds(pl.multiple_of(bb * SEQ + r * CH, CH), CH)
        lhs = jnp.concatenate([k_ref[krows, :], kaux_ref[0, rows, :]], axis=1)
        after = 0 if t == 0 else jnp.asarray(r > rd, jnp.int32)
        aux = qaux_ref[0, u, after]
        for c in range(2):
            s = _dot(lhs, jnp.concatenate([qt_ref[bb, 0, u, c], aux], axis=0))
            if t == 0:
                start = pl.multiple_of((tiles_per_chunk - 1 - u % tiles_per_chunk) * TQ, TQ)
                s = s + corr_ref[pl.ds(start, CH), :]
            s_ref[c, rows, :] = s
            m_ref[c] = jnp.maximum(m_ref[c], jnp.max(s, axis=0, keepdims=True))

    def softmax_chunk(g, r, par):
        s_ref, m_ref, acc_ref = bufs[par]
        bb, _ = split(g)
        for c in range(2):
            p = jnp.exp2(s_ref[c, rows_of(r), :] - m_ref[c]).astype(BF16)
            lane0 = (r * CH) % TM_IN
            vt = vt_ref[bb, 0, (r * CH) // TM_IN, :, lane0:lane0 + CH]
            acc_ref[c] += _dot(vt, p)

    def finalize(g, par):
        acc_ref = bufs[par][2]
        lam = (jnp.exp(jnp.sum(lq1_ref[...] * lk1_ref[...], axis=-1, keepdims=True))
               - jnp.exp(jnp.sum(lq2_ref[...] * lk2_ref[...], axis=-1, keepdims=True))
               + LAMBDA_INIT)
        a1 = acc_ref[0]
        a2 = acc_ref[1]
        o1 = a1[:DA_V_DIM] / a1[DA_V_DIM:DA_V_DIM + 1]
        o2 = a2[:DA_V_DIM] / a2[DA_V_DIM:DA_V_DIM + 1]
        o = o1 - lam * o2
        ms = jnp.mean(o * o, axis=0, keepdims=True)
        y = o * jax.lax.rsqrt(ms + SUBLN_EPS) * gs_ref[...] * (1.0 - LAMBDA_INIT)
        o_ref[pl.ds(pl.multiple_of(g * TQ, TQ), TQ), :] = y.T.astype(BF16)
        acc_ref[...] = jnp.zeros(acc_ref.shape, F32)

    def step(g, par, scores=True, softmax=True, epilogue=True):
        if epilogue:
            finalize(g - 2, par)
        if scores:
            bufs[par][1][...] = jnp.full((2, 1, TQ), -1e30, F32)
        for r in range(nch):
            if scores:
                scores_chunk(g, r, par)
            if softmax:
                softmax_chunk(g - 1, r, 1 - par)

    nt = ATT_BATCHES * nq
    acca_ref[...] = jnp.zeros(acca_ref.shape, F32)
    accb_ref[...] = jnp.zeros(accb_ref.shape, F32)
    step(0, 0, softmax=False, epilogue=False)
    step(1, 1, epilogue=False)

    def step_pair(j, carry):
        step(2 * j + 2, 0)
        step(2 * j + 3, 1)
        return carry

    jax.lax.fori_loop(0, nt // 2 - 1, step_pair, 0)
    step(nt, 0, scores=False)
    finalize(nt - 1, 1)


def _attention(ch, lq1, lk1, lq2, lk2, gs_col, qt, qaux, kz, kaux, vt):
    nq = SEQ // TQ
    nch = SEQ // CH
    vec = pl.BlockSpec((1, DA_HEAD_DIM), lambda h, b: (0, 0))
    nb = ATT_BATCHES
    return pl.pallas_call(
        _attn_kernel,
        grid=(DA_HEADS, BATCH // nb),
        in_specs=[
            pl.BlockSpec(memory_space=pltpu.SMEM),
            vec, vec, vec, vec,
            pl.BlockSpec((DA_V_DIM, 1), lambda h, b: (0, 0)),
            pl.BlockSpec((nb, 1, nq, 2, 128, TQ), lambda h, b: (b, h, 0, 0, 0, 0)),
            pl.BlockSpec((1, nq, 2, 128, TQ), lambda h, b: (h, 0, 0, 0, 0)),
            pl.BlockSpec((nb * SEQ, 128), lambda h, b: (b, h)),
            pl.BlockSpec((1, SEQ, 128), lambda h, b: (h, 0, 0)),
            pl.BlockSpec((nb, 1, SEQ // TM_IN, V_ROWS, TM_IN), lambda h, b: (b, h, 0, 0, 0)),
        ],
        out_specs=pl.BlockSpec((nb * SEQ, DA_V_DIM), lambda h, b: (b, h)),
        out_shape=jax.ShapeDtypeStruct((BATCH * SEQ, DA_WIDTH), BF16),
        scratch_shapes=[pltpu.VMEM((CORR_ROWS, TQ), F32),
                        pltpu.VMEM((2, SEQ, TQ), F32), pltpu.VMEM((2, SEQ, TQ), F32),
                        pltpu.VMEM((2, 1, TQ), F32), pltpu.VMEM((2, 1, TQ), F32),
                        pltpu.VMEM((2, V_ROWS, TQ), F32), pltpu.VMEM((2, V_ROWS, TQ), F32)],
        compiler_params=pltpu.CompilerParams(
            dimension_semantics=("arbitrary", "arbitrary"), vmem_limit_bytes=VMEM_LIMIT),
        name="diff_attention",
    )(ch, lq1, lk1, lq2, lk2, gs_col, qt, qaux, kz, kaux, vt)


def _seq_dft_kernel(x_ref, w1f_ref, m2f_ref, wup_ref, wdn_ref, wout_ref,
                    o_ref, wup_o_ref, wdn_o_ref, wout_o_ref,
                    w1_ref, m2_ref, xs_ref, a_ref, y_ref):
    nl = DFT_COLS // 128

    @pl.when(jnp.logical_and(pl.program_id(0) == 0, pl.program_id(1) == 0))
    def _prepare_matrices():
        w1_ref[...] = w1f_ref[...].astype(BF16)
        m2_ref[...] = m2f_ref[...].astype(BF16)

    wup_o_ref[...] = wup_ref[...].astype(BF16)
    wdn_o_ref[...] = wdn_ref[...].astype(BF16)
    wout_o_ref[...] = wout_ref[...].astype(BF16)

    def slab(v, l):
        return v[:, l * 128:(l + 1) * 128]

    for n1 in range(64):
        for p in range(2):
            v = x_ref[0, p, n1 * 64:(n1 + 1) * 64, :].astype(F32)
            for l in range(nl):
                xs_ref[p * nl + l, n1 * P1:n1 * P1 + 64, :] = slab(v, l)

    def stage1(n2, carry):
        parts = [jnp.concatenate([xs_ref[p * nl + l, pl.ds(n2, 64, stride=P1), :]
                                  for l in range(nl)], axis=1) for p in range(2)]
        d = jnp.concatenate(parts, axis=0).astype(BF16)
        a = _dot(w1_ref[...], d)
        row0 = pl.multiple_of(n2 * P2, 8)
        for l in range(nl):
            a_ref[l, pl.ds(row0, 128), :] = slab(a, l)
        return carry

    jax.lax.fori_loop(0, 64, stage1, 0, unroll=8)

    def stage2(k1, carry):
        parts = [jnp.concatenate([a_ref[l, pl.ds(k1 + 64 * p, 64, stride=P2), :]
                                  for l in range(nl)], axis=1) for p in range(2)]
        d = jnp.concatenate(parts, axis=0).astype(BF16)
        y = _dot(m2_ref[k1], d)
        row0 = pl.multiple_of(k1 * P1, 8)
        for l in range(nl):
            y_ref[l, pl.ds(row0, 64), :] = slab(y, l)
        return carry

    jax.lax.fori_loop(0, 64, stage2, 0, unroll=8)

    def stage3(k2, carry):
        row0 = pl.multiple_of(k2 * 64, 64)
        for l in range(nl):
            o_ref[pl.ds(row0, 64), l * 128:(l + 1) * 128] = (
                y_ref[l, pl.ds(k2, 64, stride=P1), :].astype(BF16))
        return carry

    jax.lax.fori_loop(0, 64, stage3, 0, unroll=8)


def _seq_dft(xcs, w1, m2, w_up, w_down, w_out):
    nh = FT_WIDTH // DFT_COLS
    nl = DFT_COLS // 128
    steps = BATCH * nh
    slab = lambda b, j: (b * nh + j, 0)

    def row_slab(w):
        return pl.BlockSpec((w.shape[0] // steps, w.shape[1]), slab)

    weights = (w_up, w_down, w_out)
    return pl.pallas_call(
        _seq_dft_kernel,
        grid=(BATCH, nh),
        in_specs=[
            pl.BlockSpec((1, 2, SEQ, DFT_COLS), lambda b, j: (b, 0, 0, j)),
            pl.BlockSpec((128, 128), lambda b, j: (0, 0)),
            pl.BlockSpec((64, 64, 128), lambda b, j: (0, 0, 0), pipeline_mode=pl.Buffered(1)),
        ] + [row_slab(w) for w in weights],
        out_specs=[pl.BlockSpec((SEQ, DFT_COLS), lambda b, j: (b, j))]
        + [row_slab(w) for w in weights],
        out_shape=[jax.ShapeDtypeStruct((BATCH * SEQ, FT_WIDTH), BF16)]
        + [jax.ShapeDtypeStruct(w.shape, BF16) for w in weights],
        scratch_shapes=[pltpu.VMEM((128, 128), BF16), pltpu.VMEM((64, 64, 128), BF16),
                        pltpu.VMEM((2 * nl, 64 * P1, 128), F32),
                        pltpu.VMEM((nl, 64 * P2, 128), F32),
                        pltpu.VMEM((nl, 64 * P1, 128), F32)],
        compiler_params=pltpu.CompilerParams(
            dimension_semantics=("arbitrary", "arbitrary"), vmem_limit_bytes=VMEM_LIMIT),
        name="seq_dft",
    )(xcs, w1, m2, w_up, w_down, w_out)


def _rms(x, g, eps):
    ms = jnp.mean(x * x, axis=-1, keepdims=True)
    return x * jax.lax.rsqrt(ms + eps) * g


def _ffn_kernel(xc_ref, xp_ref, xn_ref, dac_ref, dap_ref, dan_ref, ftc_ref, ftp_ref, ftn_ref,
                wo_ref, gf_ref, wup_ref, wcv_ref, bcv_ref, wdn_ref, gl_ref, o_ref, act_ref):
    i = pl.program_id(0)
    tiles_per_seq = SEQ // TM_FFN
    pos = i % tiles_per_seq
    xe = jnp.concatenate([xp_ref[...], xc_ref[...], xn_ref[...]], axis=0)
    da = jnp.concatenate([dap_ref[...], dac_ref[...], dan_ref[...]], axis=0)
    ft = jnp.concatenate([ftp_ref[...], ftc_ref[...], ftn_ref[...]], axis=0)
    x1 = xe + _dot(da, wo_ref[0:DA_WIDTH, :]) + _dot(ft, wo_ref[DA_WIDTH:, :])
    h2 = _rms(x1, gf_ref[...], EPS)
    r = jax.lax.broadcasted_iota(jnp.int32, (TM_FFN + 2 * HALO, 1), 0)
    valid = jnp.logical_and(jnp.logical_or(r >= HALO, pos > 0),
                            jnp.logical_or(r < TM_FFN + HALO, pos < tiles_per_seq - 1))
    h2 = jnp.where(valid, h2, 0.0).astype(BF16)
    n_ext = TM_FFN + 2 * HALO

    def conv(up, col0):
        w = wcv_ref[:, col0:col0 + FF_CHUNK]
        prev = pltpu.roll(up, 1, 0)[HALO:HALO + TM_FFN]
        nxt = pltpu.roll(up, n_ext - 1, 0)[HALO:HALO + TM_FFN]
        cur = up[HALO:HALO + TM_FFN]
        return (w[0:1] * prev + w[1:2] * cur + w[2:3] * nxt
                + bcv_ref[:, col0:col0 + FF_CHUNK])

    for j in range(D_FF // FF_CHUNK):
        c0 = j * FF_CHUNK
        gate = conv(_dot(h2, wup_ref[:, c0:c0 + FF_CHUNK]), c0)
        val = conv(_dot(h2, wup_ref[:, D_FF + c0:D_FF + c0 + FF_CHUNK]), D_FF + c0)
        act = gate / (1.0 + jnp.exp(-gate)) * val
        act_ref[:, c0:c0 + FF_CHUNK] = act.astype(BF16)

    y = x1[HALO:HALO + TM_FFN] + _dot(act_ref[...], wdn_ref[...])
    o_ref[...] = _rms(y, gl_ref[...], EPS)


def _ffn(x2, o_da, o_ft, w_out, g_ffn, w_up, w_conv, b_conv, w_down, g_final):
    n = BATCH * SEQ // TM_FFN
    hb = TM_FFN // HALO
    last = BATCH * SEQ // HALO - 1
    cur = lambda i: (i, 0)
    prv = lambda i: (jnp.maximum(i * hb - 1, 0), 0)
    nxt = lambda i: (jnp.minimum((i + 1) * hb, last), 0)
    const = lambda i: (0, 0)

    def trio(width):
        return [pl.BlockSpec((TM_FFN, width), cur), pl.BlockSpec((HALO, width), prv),
                pl.BlockSpec((HALO, width), nxt)]

    def resident(shape):
        return pl.BlockSpec(shape, const, pipeline_mode=pl.Buffered(1))

    return pl.pallas_call(
        _ffn_kernel,
        grid=(n,),
        in_specs=trio(D_MODEL) + trio(DA_WIDTH) + trio(FT_WIDTH) + [
            resident((D_MODEL, D_MODEL)),
            resident((1, D_MODEL)),
            resident((D_MODEL, 2 * D_FF)),
            resident((3, 2 * D_FF)),
            resident((1, 2 * D_FF)),
            resident((D_FF, D_MODEL)),
            resident((1, D_MODEL)),
        ],
        out_specs=pl.BlockSpec((TM_FFN, D_MODEL), cur),
        out_shape=jax.ShapeDtypeStruct((BATCH * SEQ, D_MODEL), F32),
        scratch_shapes=[pltpu.VMEM((TM_FFN, D_FF), BF16)],
        compiler_params=pltpu.CompilerParams(
            dimension_semantics=("arbitrary",), vmem_limit_bytes=VMEM_LIMIT),
        name="out_proj_ffn",
    )(x2, x2, x2, o_da, o_da, o_da, o_ft, o_ft, o_ft,
      w_out, g_ffn, w_up, w_conv, b_conv, w_down, g_final)


@functools.lru_cache(maxsize=None)
def _dft_tables():
    i64 = np.arange(64, dtype=np.int64)
    a1 = 2.0 * np.pi * ((i64[:, None] * i64[None, :]) % 64) / 64.0
    c1, s1 = np.cos(a1), np.sin(a1)
    w1 = np.block([[c1, -s1], [-s1, -c1]])
    kk = i64[:, None, None] + 64 * i64[None, :, None]
    a2 = 2.0 * np.pi * ((kk * i64[None, None, :]) % SEQ) / SEQ
    norm = 1.0 / math.sqrt(SEQ)
    m2 = np.concatenate([np.cos(a2) * norm, np.sin(a2) * norm], axis=2)
    c = np.arange(FT_GROUP_DIM, dtype=np.int64)
    ac = 2.0 * np.pi * ((c[:, None] * c[None, :]) % FT_GROUP_DIM) / FT_GROUP_DIM
    cn = 1.0 / math.sqrt(FT_GROUP_DIM)
    cs = np.concatenate([np.cos(ac) * cn, np.sin(ac) * cn], axis=1)
    f = lambda a: np.asarray(a, dtype=np.float32)
    return f(w1), f(m2), f(cs)


def _slope_log2e(h):
    return np.float32(LOG2E * 2.0 ** (-8.0 * (h + 1) / DA_HEADS))


def _bf16_pieces(x, n):
    pieces, rem = [], np.asarray(x, dtype=np.float64)
    for _ in range(n):
        u = rem.astype(np.float32).view(np.uint32)
        u = (u + np.uint32(0x7FFF) + ((u >> np.uint32(16)) & np.uint32(1))) & np.uint32(0xFFFF0000)
        p = u.view(np.float32).astype(np.float64)
        pieces.append(p)
        rem = rem - p
    return pieces


@functools.lru_cache(maxsize=None)
def _alibi_tables():
    pos = np.arange(SEQ, dtype=np.float64)
    kaux = np.zeros((DA_HEADS, SEQ, 128), np.float64)
    qaux = np.zeros((DA_HEADS, 2, 128, SEQ), np.float64)
    for h in range(DA_HEADS):
        pieces = _bf16_pieces(np.float64(_slope_log2e(h)) * pos, N_SPLIT)
        for t, p in enumerate(pieces):
            kaux[h, :, t] = p
            qaux[h, 0, N_SPLIT + t, :] = -p
        kaux[h, :, N_SPLIT:2 * N_SPLIT] = 1.0
        qaux[h, 0, 0:N_SPLIT, :] = 1.0
        qaux[h, 1] = -qaux[h, 0]
    qaux = qaux.reshape(DA_HEADS, 2, 128, SEQ // TQ, TQ).transpose(0, 3, 1, 2, 4)
    return np.asarray(kaux, dtype=BF16), np.ascontiguousarray(qaux).astype(BF16)


def kernel(x, g_mix, w_in, lambda_q1, lambda_k1, lambda_q2, lambda_k2, g_subln, w_ft, w_out,
           g_ffn, w_up, w_conv, b_conv, w_down, g_final):
    x2 = x.reshape(BATCH * SEQ, D_MODEL)
    w1, m2, cs = _dft_tables()
    qt, kz, vt, xcs = _in_proj(x2, g_mix[0][None, :], w_in[0], jnp.asarray(cs), w_ft[0])

    ch = jnp.asarray([_slope_log2e(h) for h in range(DA_HEADS)], F32)
    kaux, qaux = _alibi_tables()
    o_da = _attention(ch, lambda_q1, lambda_k1, lambda_q2, lambda_k2, g_subln[0][:, None],
                      qt, jnp.asarray(qaux), kz, jnp.asarray(kaux), vt)

    o_ft, w_up_bf, w_down_bf, w_out_bf = _seq_dft(xcs, jnp.asarray(w1), jnp.asarray(m2),
                                                  w_up[0], w_down[0], w_out[0])

    y = _ffn(x2, o_da, o_ft, w_out_bf, g_ffn[0][None, :], w_up_bf,
             w_conv[0], b_conv[0][None, :], w_down_bf, g_final[None, :])
    return y.reshape(BATCH, SEQ, D_MODEL)
```

```python
import functools
import math

import numpy as np
import jax
import jax.numpy as jnp
from jax.experimental import pallas as pl
from jax.experimental.pallas import tpu as pltpu

D_MODEL = 1024
BATCH = 8
SEQ = 4096
DA_HEADS = 4
DA_HEAD_DIM = 64
DA_V_DIM = 128
QK_WIDTH = 512
DA_WIDTH = 512
FT_GROUPS = 4
FT_GROUP_DIM = 128
FT_WIDTH = 512
IN_PROJ_WIDTH = 2048
D_FF = 2816
EPS = 1e-6
SUBLN_EPS = 1e-5
LAMBDA_INIT = 0.8 - 0.6 * math.exp(-0.3 * 0)
LOG2E = 1.4426950408889634

BF16 = jnp.bfloat16
F32 = jnp.float32

TM_IN = 1024
TQ = 256
CH = 512
ATT_BATCHES = 2
CORR_ROWS = CH + (CH // TQ - 1) * TQ
N_SPLIT = 4
V_ROWS = DA_V_DIM + 16
DFT_COLS = 256
P1 = 72
P2 = 136
TM_FFN = 1024
HALO = 16
FF_CHUNK = 256
VMEM_LIMIT = 56 * 1024 * 1024


def _dot(a, b):
    return jnp.dot(a, b, preferred_element_type=F32)


def _in_proj_kernel(x_ref, g_ref, w_ref, cs_ref, wft_ref, qt_ref, k_ref, vt_ref, xcs_ref,
                    wbf_ref, csw_ref):
    @pl.when(jnp.logical_and(pl.program_id(0) == 0, pl.program_id(1) == 0))
    def _prepare_weights():
        wbf_ref[...] = w_ref[...].astype(BF16)
        cs = cs_ref[...].astype(BF16)
        for g in range(FT_GROUPS):
            wg = wft_ref[g].astype(BF16)
            csw_ref[g] = jnp.concatenate([_dot(cs[:, :128], wg), _dot(cs[:, 128:], wg)],
                                         axis=1).astype(BF16)

    x = x_ref[...]
    ms = jnp.mean(x * x, axis=-1, keepdims=True)
    h = (x * jax.lax.rsqrt(ms + EPS) * g_ref[...]).astype(BF16)
    z = _dot(h, wbf_ref[...])
    row = jax.lax.broadcasted_iota(jnp.int32, (2 * DA_HEAD_DIM, TM_IN), 0)
    for hd in range(DA_HEADS):
        zq_t = (z[:, hd * 128:(hd + 1) * 128] * (DA_HEAD_DIM ** -0.5 * LOG2E)).T
        q1 = jnp.where(row < DA_HEAD_DIM, zq_t, 0.0).astype(BF16)
        q2 = jnp.where(row >= DA_HEAD_DIM, zq_t, 0.0).astype(BF16)
        for j in range(TM_IN // TQ):
            qt_ref[0, hd, j, 0] = q1[:, j * TQ:(j + 1) * TQ]
            qt_ref[0, hd, j, 1] = q2[:, j * TQ:(j + 1) * TQ]
        zv_t = z[:, 2 * QK_WIDTH + hd * 128:2 * QK_WIDTH + (hd + 1) * 128].T
        vt_ref[0, hd, 0, 0:DA_V_DIM, :] = zv_t.astype(BF16)
        ones_row = jax.lax.broadcasted_iota(jnp.int32, (V_ROWS - DA_V_DIM, TM_IN), 0) == 0
        vt_ref[0, hd, 0, DA_V_DIM:V_ROWS, :] = jnp.where(ones_row, 1.0, 0.0).astype(BF16)
    k_ref[...] = z[:, QK_WIDTH:2 * QK_WIDTH].astype(BF16)
    u0 = 2 * QK_WIDTH + DA_WIDTH
    for g in range(FT_GROUPS):
        ug = z[:, u0 + g * 128:u0 + (g + 1) * 128].astype(BF16)
        xcs = _dot(ug, csw_ref[g])
        xcs_ref[0, 0, :, g * 128:(g + 1) * 128] = xcs[:, :128].astype(BF16)
        xcs_ref[0, 1, :, g * 128:(g + 1) * 128] = xcs[:, 128:].astype(BF16)


def _in_proj(x2, g_mix, w_in, cs, w_ft):
    nt = SEQ // TM_IN
    return pl.pallas_call(
        _in_proj_kernel,
        grid=(BATCH, nt),
        in_specs=[
            pl.BlockSpec((TM_IN, D_MODEL), lambda b, t: (b * nt + t, 0)),
            pl.BlockSpec((1, D_MODEL), lambda b, t: (0, 0)),
            pl.BlockSpec((D_MODEL, IN_PROJ_WIDTH), lambda b, t: (0, 0),
                         pipeline_mode=pl.Buffered(1)),
            pl.BlockSpec((FT_GROUP_DIM, 2 * FT_GROUP_DIM), lambda b, t: (0, 0)),
            pl.BlockSpec((FT_GROUPS, FT_GROUP_DIM, FT_GROUP_DIM), lambda b, t: (0, 0, 0)),
        ],
        out_specs=[
            pl.BlockSpec((1, DA_HEADS, TM_IN // TQ, 2, 128, TQ),
                         lambda b, t: (b, 0, t, 0, 0, 0)),
            pl.BlockSpec((TM_IN, QK_WIDTH), lambda b, t: (b * nt + t, 0)),
            pl.BlockSpec((1, DA_HEADS, 1, V_ROWS, TM_IN), lambda b, t: (b, 0, t, 0, 0)),
            pl.BlockSpec((1, 2, TM_IN, FT_WIDTH), lambda b, t: (b, 0, t, 0)),
        ],
        out_shape=[
            jax.ShapeDtypeStruct((BATCH, DA_HEADS, SEQ // TQ, 2, 128, TQ), BF16),
            jax.ShapeDtypeStruct((BATCH * SEQ, QK_WIDTH), BF16),
            jax.ShapeDtypeStruct((BATCH, DA_HEADS, nt, V_ROWS, TM_IN), BF16),
            jax.ShapeDtypeStruct((BATCH, 2, SEQ, FT_WIDTH), BF16),
        ],
        scratch_shapes=[pltpu.VMEM((D_MODEL, IN_PROJ_WIDTH), BF16),
                        pltpu.VMEM((FT_GROUPS, FT_GROUP_DIM, 2 * FT_GROUP_DIM), BF16)],
        compiler_params=pltpu.CompilerParams(
            dimension_semantics=("arbitrary", "arbitrary"), vmem_limit_bytes=VMEM_LIMIT),
        name="in_proj",
    )(x2, g_mix, w_in, cs, w_ft)


def _attn_kernel(ch_ref, lq1_ref, lk1_ref, lq2_ref, lk2_ref, gs_ref,
                 qt_ref, qaux_ref, k_ref, kaux_ref, vt_ref, o_ref, corr_ref,
                 sa_ref, pad_ref, sb_ref, ma_ref, mb_ref, acca_ref, accb_ref):
    hd = pl.program_id(0)
    bufs = ((sa_ref, ma_ref, acca_ref), (sb_ref, mb_ref, accb_ref))
    nch = SEQ // CH
    nq = SEQ // TQ
    tiles_per_chunk = CH // TQ

    @pl.when(pl.program_id(1) == 0)
    def _build_corr():
        n2slope = -2.0 * ch_ref[hd]
        d = (jax.lax.broadcasted_iota(jnp.int32, (TQ, TQ), 0)
             - jax.lax.broadcasted_iota(jnp.int32, (TQ, TQ), 1))
        for i in range(CORR_ROWS // TQ):
            off = (i - (tiles_per_chunk - 1)) * TQ
            corr_ref[i * TQ:(i + 1) * TQ, :] = jnp.maximum(d + off, 0).astype(F32) * n2slope

    def rows_of(r):
        return slice(r * CH, (r + 1) * CH)

    def split(g):
        return g // nq, g % nq

    def scores_chunk(g, t, par):
        s_ref, m_ref, _ = bufs[par]
        bb, u = split(g)
        rd = u // tiles_per_chunk
        r = (rd + t) % nch
        rows = pl.ds(pl.multiple_of(r * CH, CH), CH)
        krows = pl.ds(pl.multiple_of(bb * SEQ + r * CH, CH), CH)
        lhs = jnp.concatenate([k_ref[krows, :], kaux_ref[0, rows, :]], axis=1)
        after = 0 if t == 0 else jnp.asarray(r > rd, jnp.int32)
        aux = qaux_ref[0, u, after]
        for c in range(2):
            s = _dot(lhs, jnp.concatenate([qt_ref[bb, 0, u, c], aux], axis=0))
            if t == 0:
                start = pl.multiple_of((tiles_per_chunk - 1 - u % tiles_per_chunk) * TQ, TQ)
                s = s + corr_ref[pl.ds(start, CH), :]
            s_ref[c, rows, :] = s
            m_ref[c] = jnp.maximum(m_ref[c], jnp.max(s, axis=0, keepdims=True))

    def softmax_chunk(g, r, par):
        s_ref, m_ref, acc_ref = bufs[par]
        bb, _ = split(g)
        for c in range(2):
            p = jnp.exp2(s_ref[c, rows_of(r), :] - m_ref[c]).astype(BF16)
            lane0 = (r * CH) % TM_IN
            vt = vt_ref[bb, 0, (r * CH) // TM_IN, :, lane0:lane0 + CH]
            acc_ref[c] += _dot(vt, p)

    def finalize(g, par):
        acc_ref = bufs[par][2]
        lam = (jnp.exp(jnp.sum(lq1_ref[...] * lk1_ref[...], axis=-1, keepdims=True))
               - jnp.exp(jnp.sum(lq2_ref[...] * lk2_ref[...], axis=-1, keepdims=True))
               + LAMBDA_INIT)
        a1 = acc_ref[0]
        a2 = acc_ref[1]
        o1 = a1[:DA_V_DIM] / a1[DA_V_DIM:DA_V_DIM + 1]
        o2 = a2[:DA_V_DIM] / a2[DA_V_DIM:DA_V_DIM + 1]
        o = o1 - lam * o2
        ms = jnp.mean(o * o, axis=0, keepdims=True)
        y = o * jax.lax.rsqrt(ms + SUBLN_EPS) * gs_ref[...] * (1.0 - LAMBDA_INIT)
        o_ref[pl.ds(pl.multiple_of(g * TQ, TQ), TQ), :] = y.T.astype(BF16)
        acc_ref[...] = jnp.zeros(acc_ref.shape, F32)

    def step(g, par, scores=True, softmax=True, epilogue=True):
        if epilogue:
            finalize(g - 2, par)
        if scores:
            bufs[par][1][...] = jnp.full((2, 1, TQ), -1e30, F32)
        for r in range(nch):
            if scores:
                scores_chunk(g, r, par)
            if softmax:
                softmax_chunk(g - 1, r, 1 - par)

    nt = ATT_BATCHES * nq
    acca_ref[...] = jnp.zeros(acca_ref.shape, F32)
    accb_ref[...] = jnp.zeros(accb_ref.shape, F32)
    step(0, 0, softmax=False, epilogue=False)
    step(1, 1, epilogue=False)

    def step_pair(j, carry):
        step(2 * j + 2, 0)
        step(2 * j + 3, 1)
        return carry

    jax.lax.fori_loop(0, nt // 2 - 1, step_pair, 0)
    step(nt, 0, scores=False)
    finalize(nt - 1, 1)


def _attention(ch, lq1, lk1, lq2, lk2, gs_col, qt, qaux, kz, kaux, vt):
    nq = SEQ // TQ
    nch = SEQ // CH
    vec = pl.BlockSpec((1, DA_HEAD_DIM), lambda h, b: (0, 0))
    nb = ATT_BATCHES
    return pl.pallas_call(
        _attn_kernel,
        grid=(DA_HEADS, BATCH // nb),
        in_specs=[
            pl.BlockSpec(memory_space=pltpu.SMEM),
            vec, vec, vec, vec,
            pl.BlockSpec((DA_V_DIM, 1), lambda h, b: (0, 0)),
            pl.BlockSpec((nb, 1, nq, 2, 128, TQ), lambda h, b: (b, h, 0, 0, 0, 0)),
            pl.BlockSpec((1, nq, 2, 128, TQ), lambda h, b: (h, 0, 0, 0, 0)),
            pl.BlockSpec((nb * SEQ, 128), lambda h, b: (b, h)),
            pl.BlockSpec((1, SEQ, 128), lambda h, b: (h, 0, 0)),
            pl.BlockSpec((nb, 1, SEQ // TM_IN, V_ROWS, TM_IN), lambda h, b: (b, h, 0, 0, 0)),
        ],
        out_specs=pl.BlockSpec((nb * SEQ, DA_V_DIM), lambda h, b: (b, h)),
        out_shape=jax.ShapeDtypeStruct((BATCH * SEQ, DA_WIDTH), BF16),
        scratch_shapes=[pltpu.VMEM((CORR_ROWS, TQ), F32),
                        pltpu.VMEM((2, SEQ, TQ), F32),
                        pltpu.VMEM((8, TQ), F32),
                        pltpu.VMEM((2, SEQ, TQ), F32),
                        pltpu.VMEM((2, 1, TQ), F32), pltpu.VMEM((2, 1, TQ), F32),
                        pltpu.VMEM((2, V_ROWS, TQ), F32), pltpu.VMEM((2, V_ROWS, TQ), F32)],
        compiler_params=pltpu.CompilerParams(
            dimension_semantics=("arbitrary", "arbitrary"), vmem_limit_bytes=VMEM_LIMIT),
        name="diff_attention",
    )(ch, lq1, lk1, lq2, lk2, gs_col, qt, qaux, kz, kaux, vt)


def _seq_dft_kernel(x_ref, w1f_ref, m2f_ref, wup_ref, wdn_ref, wout_ref,
                    o_ref, wup_o_ref, wdn_o_ref, wout_o_ref,
                    w1_ref, m2_ref, xs_ref, a_ref, y_ref):
    nl = DFT_COLS // 128

    @pl.when(jnp.logical_and(pl.program_id(0) == 0, pl.program_id(1) == 0))
    def _prepare_matrices():
        w1_ref[...] = w1f_ref[...].astype(BF16)
        m2_ref[...] = m2f_ref[...].astype(BF16)

    wup_o_ref[...] = wup_ref[...].astype(BF16)
    wdn_o_ref[...] = wdn_ref[...].astype(BF16)
    wout_o_ref[...] = wout_ref[...].astype(BF16)

    def slab(v, l):
        return v[:, l * 128:(l + 1) * 128]

    for n1 in range(64):
        for p in range(2):
            v = x_ref[0, p, n1 * 64:(n1 + 1) * 64, :].astype(F32)
            for l in range(nl):
                xs_ref[p * nl + l, n1 * P1:n1 * P1 + 64, :] = slab(v, l)

    def stage1(n2, carry):
        parts = [jnp.concatenate([xs_ref[p * nl + l, pl.ds(n2, 64, stride=P1), :]
                                  for l in range(nl)], axis=1) for p in range(2)]
        d = jnp.concatenate(parts, axis=0).astype(BF16)
        a = _dot(w1_ref[...], d)
        row0 = pl.multiple_of(n2 * P2, 8)
        for l in range(nl):
            a_ref[l, pl.ds(row0, 128), :] = slab(a, l)
        return carry

    jax.lax.fori_loop(0, 64, stage1, 0, unroll=8)

    def stage2(k1, carry):
        parts = [jnp.concatenate([a_ref[l, pl.ds(k1 + 64 * p, 64, stride=P2), :]
                                  for l in range(nl)], axis=1) for p in range(2)]
        d = jnp.concatenate(parts, axis=0).astype(BF16)
        y = _dot(m2_ref[k1], d)
        row0 = pl.multiple_of(k1 * P1, 8)
        for l in range(nl):
            y_ref[l, pl.ds(row0, 64), :] = slab(y, l)
        return carry

    jax.lax.fori_loop(0, 64, stage2, 0, unroll=8)

    def stage3(k2, carry):
        row0 = pl.multiple_of(k2 * 64, 64)
        for l in range(nl):
            o_ref[pl.ds(row0, 64), l * 128:(l + 1) * 128] = (
                y_ref[l, pl.ds(k2, 64, stride=P1), :].astype(BF16))
        return carry

    jax.lax.fori_loop(0, 64, stage3, 0, unroll=8)


def _seq_dft(xcs, w1, m2, w_up, w_down, w_out):
    nh = FT_WIDTH // DFT_COLS
    nl = DFT_COLS // 128
    steps = BATCH * nh
    slab = lambda b, j: (b * nh + j, 0)

    def row_slab(w):
        return pl.BlockSpec((w.shape[0] // steps, w.shape[1]), slab)

    weights = (w_up, w_down, w_out)
    return pl.pallas_call(
        _seq_dft_kernel,
        grid=(BATCH, nh),
        in_specs=[
            pl.BlockSpec((1, 2, SEQ, DFT_COLS), lambda b, j: (b, 0, 0, j)),
            pl.BlockSpec((128, 128), lambda b, j: (0, 0)),
            pl.BlockSpec((64, 64, 128), lambda b, j: (0, 0, 0), pipeline_mode=pl.Buffered(1)),
        ] + [row_slab(w) for w in weights],
        out_specs=[pl.BlockSpec((SEQ, DFT_COLS), lambda b, j: (b, j))]
        + [row_slab(w) for w in weights],
        out_shape=[jax.ShapeDtypeStruct((BATCH * SEQ, FT_WIDTH), BF16)]
        + [jax.ShapeDtypeStruct(w.shape, BF16) for w in weights],
        scratch_shapes=[pltpu.VMEM((128, 128), BF16), pltpu.VMEM((64, 64, 128), BF16),
                        pltpu.VMEM((2 * nl, 64 * P1, 128), F32),
                        pltpu.VMEM((nl, 64 * P2, 128), F32),
                        pltpu.VMEM((nl, 64 * P1, 128), F32)],
        compiler_params=pltpu.CompilerParams(
            dimension_semantics=("arbitrary", "arbitrary"), vmem_limit_bytes=VMEM_LIMIT),
        name="seq_dft",
    )(xcs, w1, m2, w_up, w_down, w_out)


def _rms(x, g, eps):
    ms = jnp.mean(x * x, axis=-1, keepdims=True)
    return x * jax.lax.rsqrt(ms + eps) * g


def _ffn_kernel(xc_ref, xp_ref, xn_ref, dac_ref, dap_ref, dan_ref, ftc_ref, ftp_ref, ftn_ref,
                wo_ref, gf_ref, wup_ref, wcv_ref, bcv_ref, wdn_ref, gl_ref, o_ref, act_ref):
    i = pl.program_id(0)
    tiles_per_seq = SEQ // TM_FFN
    pos = i % tiles_per_seq
    xe = jnp.concatenate([xp_ref[...], xc_ref[...], xn_ref[...]], axis=0)
    da = jnp.concatenate([dap_ref[...], dac_ref[...], dan_ref[...]], axis=0)
    ft = jnp.concatenate([ftp_ref[...], ftc_ref[...], ftn_ref[...]], axis=0)
    x1 = xe + _dot(da, wo_ref[0:DA_WIDTH, :]) + _dot(ft, wo_ref[DA_WIDTH:, :])
    h2 = _rms(x1, gf_ref[...], EPS)
    r = jax.lax.broadcasted_iota(jnp.int32, (TM_FFN + 2 * HALO, 1), 0)
    valid = jnp.logical_and(jnp.logical_or(r >= HALO, pos > 0),
                            jnp.logical_or(r < TM_FFN + HALO, pos < tiles_per_seq - 1))
    h2 = jnp.where(valid, h2, 0.0).astype(BF16)
    n_ext = TM_FFN + 2 * HALO

    def conv(up, col0):
        w = wcv_ref[:, col0:col0 + FF_CHUNK]
        prev = pltpu.roll(up, 1, 0)[HALO:HALO + TM_FFN]
        nxt = pltpu.roll(up, n_ext - 1, 0)[HALO:HALO + TM_FFN]
        cur = up[HALO:HALO + TM_FFN]
        return (w[0:1] * prev + w[1:2] * cur + w[2:3] * nxt
                + bcv_ref[:, col0:col0 + FF_CHUNK])

    for j in range(D_FF // FF_CHUNK):
        c0 = j * FF_CHUNK
        gate = conv(_dot(h2, wup_ref[:, c0:c0 + FF_CHUNK]), c0)
        val = conv(_dot(h2, wup_ref[:, D_FF + c0:D_FF + c0 + FF_CHUNK]), D_FF + c0)
        act = gate / (1.0 + jnp.exp(-gate)) * val
        act_ref[:, c0:c0 + FF_CHUNK] = act.astype(BF16)

    y = x1[HALO:HALO + TM_FFN] + _dot(act_ref[...], wdn_ref[...])
    o_ref[...] = _rms(y, gl_ref[...], EPS)


def _ffn(x2, o_da, o_ft, w_out, g_ffn, w_up, w_conv, b_conv, w_down, g_final):
    n = BATCH * SEQ // TM_FFN
    hb = TM_FFN // HALO
    last = BATCH * SEQ // HALO - 1
    cur = lambda i: (i, 0)
    prv = lambda i: (jnp.maximum(i * hb - 1, 0), 0)
    nxt = lambda i: (jnp.minimum((i + 1) * hb, last), 0)
    const = lambda i: (0, 0)

    def trio(width):
        return [pl.BlockSpec((TM_FFN, width), cur), pl.BlockSpec((HALO, width), prv),
                pl.BlockSpec((HALO, width), nxt)]

    def resident(shape):
        return pl.BlockSpec(shape, const, pipeline_mode=pl.Buffered(1))

    return pl.pallas_call(
        _ffn_kernel,
        grid=(n,),
        in_specs=trio(D_MODEL) + trio(DA_WIDTH) + trio(FT_WIDTH) + [
            resident((D_MODEL, D_MODEL)),
            resident((1, D_MODEL)),
            resident((D_MODEL, 2 * D_FF)),
            resident((3, 2 * D_FF)),
            resident((1, 2 * D_FF)),
            resident((D_FF, D_MODEL)),
            resident((1, D_MODEL)),
        ],
        out_specs=pl.BlockSpec((TM_FFN, D_MODEL), cur),
        out_shape=jax.ShapeDtypeStruct((BATCH * SEQ, D_MODEL), F32),
        scratch_shapes=[pltpu.VMEM((TM_FFN, D_FF), BF16)],
        compiler_params=pltpu.CompilerParams(
            dimension_semantics=("arbitrary",), vmem_limit_bytes=VMEM_LIMIT),
        name="out_proj_ffn",
    )(x2, x2, x2, o_da, o_da, o_da, o_ft, o_ft, o_ft,
      w_out, g_ffn, w_up, w_conv, b_conv, w_down, g_final)


@functools.lru_cache(maxsize=None)
def _dft_tables():
    i64 = np.arange(64, dtype=np.int64)
    a1 = 2.0 * np.pi * ((i64[:, None] * i64[None, :]) % 64) / 64.0
    c1, s1 = np.cos(a1), np.sin(a1)
    w1 = np.block([[c1, -s1], [-s1, -c1]])
    kk = i64[:, None, None] + 64 * i64[None, :, None]
    a2 = 2.0 * np.pi * ((kk * i64[None, None, :]) % SEQ) / SEQ
    norm = 1.0 / math.sqrt(SEQ)
    m2 = np.concatenate([np.cos(a2) * norm, np.sin(a2) * norm], axis=2)
    c = np.arange(FT_GROUP_DIM, dtype=np.int64)
    ac = 2.0 * np.pi * ((c[:, None] * c[None, :]) % FT_GROUP_DIM) / FT_GROUP_DIM
    cn = 1.0 / math.sqrt(FT_GROUP_DIM)
    cs = np.concatenate([np.cos(ac) * cn, np.sin(ac) * cn], axis=1)
    f = lambda a: np.asarray(a, dtype=np.float32)
    return f(w1), f(m2), f(cs)


def _slope_log2e(h):
    return np.float32(LOG2E * 2.0 ** (-8.0 * (h + 1) / DA_HEADS))


def _bf16_pieces(x, n):
    pieces, rem = [], np.asarray(x, dtype=np.float64)
    for _ in range(n):
        u = rem.astype(np.float32).view(np.uint32)
        u = (u + np.uint32(0x7FFF) + ((u >> np.uint32(16)) & np.uint32(1))) & np.uint32(0xFFFF0000)
        p = u.view(np.float32).astype(np.float64)
        pieces.append(p)
        rem = rem - p
    return pieces


@functools.lru_cache(maxsize=None)
def _alibi_tables():
    pos = np.arange(SEQ, dtype=np.float64)
    kaux = np.zeros((DA_HEADS, SEQ, 128), np.float64)
    qaux = np.zeros((DA_HEADS, 2, 128, SEQ), np.float64)
    for h in range(DA_HEADS):
        pieces = _bf16_pieces(np.float64(_slope_log2e(h)) * pos, N_SPLIT)
        for t, p in enumerate(pieces):
            kaux[h, :, t] = p
            qaux[h, 0, N_SPLIT + t, :] = -p
        kaux[h, :, N_SPLIT:2 * N_SPLIT] = 1.0
        qaux[h, 0, 0:N_SPLIT, :] = 1.0
        qaux[h, 1] = -qaux[h, 0]
    qaux = qaux.reshape(DA_HEADS, 2, 128, SEQ // TQ, TQ).transpose(0, 3, 1, 2, 4)
    return np.asarray(kaux, dtype=BF16), np.ascontiguousarray(qaux).astype(BF16)


def kernel(x, g_mix, w_in, lambda_q1, lambda_k1, lambda_q2, lambda_k2, g_subln, w_ft, w_out,
           g_ffn, w_up, w_conv, b_conv, w_down, g_final):
    x2 = x.reshape(BATCH * SEQ, D_MODEL)
    w1, m2, cs = _dft_tables()
    qt, kz, vt, xcs = _in_proj(x2, g_mix[0][None, :], w_in[0], jnp.asarray(cs), w_ft[0])

    ch = jnp.asarray([_slope_log2e(h) for h in range(DA_HEADS)], F32)
    kaux, qaux = _alibi_tables()
    o_da = _attention(ch, lambda_q1, lambda_k1, lambda_q2, lambda_k2, g_subln[0][:, None],
                      qt, jnp.asarray(qaux), kz, jnp.asarray(kaux), vt)

    o_ft, w_up_bf, w_down_bf, w_out_bf = _seq_dft(xcs, jnp.asarray(w1), jnp.asarray(m2),
                                                  w_up[0], w_down[0], w_out[0])

    y = _ffn(x2, o_da, o_ft, w_out_bf, g_ffn[0][None, :], w_up_bf,
             w_conv[0], b_conv[0][None, :], w_down_bf, g_final[None, :])
    return y.reshape(BATCH, SEQ, D_MODEL)
```

```python
import functools
import math

import numpy as np
import jax
import jax.numpy as jnp
from jax.experimental import pallas as pl
from jax.experimental.pallas import tpu as pltpu

D_MODEL = 1024
BATCH = 8
SEQ = 4096
DA_HEADS = 4
DA_HEAD_DIM = 64
DA_V_DIM = 128
QK_WIDTH = 512
DA_WIDTH = 512
FT_GROUPS = 4
FT_GROUP_DIM = 128
FT_WIDTH = 512
IN_PROJ_WIDTH = 2048
D_FF = 2816
EPS = 1e-6
SUBLN_EPS = 1e-5
LAMBDA_INIT = 0.8 - 0.6 * math.exp(-0.3 * 0)
LOG2E = 1.4426950408889634

BF16 = jnp.bfloat16
F32 = jnp.float32

TM_IN = 1024
TQ = 256
CH = 512
ATT_BATCHES = 2
CORR_ROWS = CH + (CH // TQ - 1) * TQ
N_SPLIT = 4
V_ROWS = DA_V_DIM + 16
DFT_COLS = 256
P1 = 72
P2 = 136
TM_FFN = 1024
HALO = 16
FF_CHUNK = 256
VMEM_LIMIT = 56 * 1024 * 1024


def _dot(a, b):
    return jnp.dot(a, b, preferred_element_type=F32)


def _in_proj_kernel(x_ref, g_ref, w_ref, cs_ref, wft_ref, qt_ref, k_ref, vt_ref, xcs_ref,
                    wbf_ref, csw_ref):
    @pl.when(jnp.logical_and(pl.program_id(0) == 0, pl.program_id(1) == 0))
    def _prepare_weights():
        wbf_ref[...] = w_ref[...].astype(BF16)
        cs = cs_ref[...].astype(BF16)
        for g in range(FT_GROUPS):
            wg = wft_ref[g].astype(BF16)
            csw_ref[g] = jnp.concatenate([_dot(cs[:, :128], wg), _dot(cs[:, 128:], wg)],
                                         axis=1).astype(BF16)

    x = x_ref[...]
    ms = jnp.mean(x * x, axis=-1, keepdims=True)
    h = (x * jax.lax.rsqrt(ms + EPS) * g_ref[...]).astype(BF16)
    z = _dot(h, wbf_ref[...])
    row = jax.lax.broadcasted_iota(jnp.int32, (2 * DA_HEAD_DIM, TM_IN), 0)
    for hd in range(DA_HEADS):
        zq_t = (z[:, hd * 128:(hd + 1) * 128] * (DA_HEAD_DIM ** -0.5 * LOG2E)).T
        q1 = jnp.where(row < DA_HEAD_DIM, zq_t, 0.0).astype(BF16)
        q2 = jnp.where(row >= DA_HEAD_DIM, zq_t, 0.0).astype(BF16)
        for j in range(TM_IN // TQ):
            qt_ref[0, hd, j, 0] = q1[:, j * TQ:(j + 1) * TQ]
            qt_ref[0, hd, j, 1] = q2[:, j * TQ:(j + 1) * TQ]
        zv_t = z[:, 2 * QK_WIDTH + hd * 128:2 * QK_WIDTH + (hd + 1) * 128].T
        vt_ref[0, hd, 0, 0:DA_V_DIM, :] = zv_t.astype(BF16)
        ones_row = jax.lax.broadcasted_iota(jnp.int32, (V_ROWS - DA_V_DIM, TM_IN), 0) == 0
        vt_ref[0, hd, 0, DA_V_DIM:V_ROWS, :] = jnp.where(ones_row, 1.0, 0.0).astype(BF16)
    k_ref[...] = z[:, QK_WIDTH:2 * QK_WIDTH].astype(BF16)
    u0 = 2 * QK_WIDTH + DA_WIDTH
    for g in range(FT_GROUPS):
        ug = z[:, u0 + g * 128:u0 + (g + 1) * 128].astype(BF16)
        xcs = _dot(ug, csw_ref[g])
        xcs_ref[0, 0, :, g * 128:(g + 1) * 128] = xcs[:, :128].astype(BF16)
        xcs_ref[0, 1, :, g * 128:(g + 1) * 128] = xcs[:, 128:].astype(BF16)


def _in_proj(x2, g_mix, w_in, cs, w_ft):
    nt = SEQ // TM_IN
    return pl.pallas_call(
        _in_proj_kernel,
        grid=(BATCH, nt),
        in_specs=[
            pl.BlockSpec((TM_IN, D_MODEL), lambda b, t: (b * nt + t, 0)),
            pl.BlockSpec((1, D_MODEL), lambda b, t: (0, 0)),
            pl.BlockSpec((D_MODEL, IN_PROJ_WIDTH), lambda b, t: (0, 0),
                         pipeline_mode=pl.Buffered(1)),
            pl.BlockSpec((FT_GROUP_DIM, 2 * FT_GROUP_DIM), lambda b, t: (0, 0)),
            pl.BlockSpec((FT_GROUPS, FT_GROUP_DIM, FT_GROUP_DIM), lambda b, t: (0, 0, 0)),
        ],
        out_specs=[
            pl.BlockSpec((1, DA_HEADS, TM_IN // TQ, 2, 128, TQ),
                         lambda b, t: (b, 0, t, 0, 0, 0)),
            pl.BlockSpec((TM_IN, QK_WIDTH), lambda b, t: (b * nt + t, 0)),
            pl.BlockSpec((1, DA_HEADS, 1, V_ROWS, TM_IN), lambda b, t: (b, 0, t, 0, 0)),
            pl.BlockSpec((1, 2, TM_IN, FT_WIDTH), lambda b, t: (b, 0, t, 0)),
        ],
        out_shape=[
            jax.ShapeDtypeStruct((BATCH, DA_HEADS, SEQ // TQ, 2, 128, TQ), BF16),
            jax.ShapeDtypeStruct((BATCH * SEQ, QK_WIDTH), BF16),
            jax.ShapeDtypeStruct((BATCH, DA_HEADS, nt, V_ROWS, TM_IN), BF16),
            jax.ShapeDtypeStruct((BATCH, 2, SEQ, FT_WIDTH), BF16),
        ],
        scratch_shapes=[pltpu.VMEM((D_MODEL, IN_PROJ_WIDTH), BF16),
                        pltpu.VMEM((FT_GROUPS, FT_GROUP_DIM, 2 * FT_GROUP_DIM), BF16)],
        compiler_params=pltpu.CompilerParams(
            dimension_semantics=("arbitrary", "arbitrary"), vmem_limit_bytes=VMEM_LIMIT),
        name="in_proj",
    )(x2, g_mix, w_in, cs, w_ft)


def _attn_kernel(ch_ref, lq1_ref, lk1_ref, lq2_ref, lk2_ref, gs_ref,
                 qt_ref, qaux_ref, k_ref, kaux_ref, vt_ref, o_ref, corr_ref,
                 sa_ref, sb_ref, ma_ref, mb_ref, acca_ref, accb_ref):
    hd = pl.program_id(0)
    bufs = ((sa_ref, ma_ref, acca_ref), (sb_ref, mb_ref, accb_ref))
    nch = SEQ // CH
    nq = SEQ // TQ
    tiles_per_chunk = CH // TQ

    @pl.when(pl.program_id(1) == 0)
    def _build_corr():
        n2slope = -2.0 * ch_ref[hd]
        d = (jax.lax.broadcasted_iota(jnp.int32, (TQ, TQ), 0)
             - jax.lax.broadcasted_iota(jnp.int32, (TQ, TQ), 1))
        for i in range(CORR_ROWS // TQ):
            off = (i - (tiles_per_chunk - 1)) * TQ
            corr_ref[i * TQ:(i + 1) * TQ, :] = jnp.maximum(d + off, 0).astype(F32) * n2slope

    def rows_of(r):
        return slice(r * CH, (r + 1) * CH)

    def split(g):
        return g // nq, g % nq

    def scores_chunk(g, t, par):
        s_ref, m_ref, _ = bufs[par]
        bb, u = split(g)
        rd = u // tiles_per_chunk
        r = (rd + t) % nch
        rows = pl.ds(pl.multiple_of(r * CH, CH), CH)
        krows = pl.ds(pl.multiple_of(bb * SEQ + r * CH, CH), CH)
        lhs = jnp.concatenate([k_ref[krows, :], kaux_ref[0, rows, :]], axis=1)
        after = 0 if t == 0 else jnp.asarray(r > rd, jnp.int32)
        aux = qaux_ref[0, u, after]
        for c in range(2):
            s = _dot(lhs, jnp.concatenate([qt_ref[bb, 0, u, c], aux], axis=0))
            if t == 0:
                start = pl.multiple_of((tiles_per_chunk - 1 - u % tiles_per_chunk) * TQ, TQ)
                s = s + corr_ref[pl.ds(start, CH), :]
            s_ref[c, rows, :] = s
            m_ref[c] = jnp.maximum(m_ref[c], jnp.max(s, axis=0, keepdims=True))

    def softmax_chunk(g, r, par):
        s_ref, m_ref, acc_ref = bufs[par]
        bb, _ = split(g)
        for c in range(2):
            p = jnp.exp2(s_ref[c, rows_of(r), :] - m_ref[c]).astype(BF16)
            lane0 = (r * CH) % TM_IN
            vt = vt_ref[bb, 0, (r * CH) // TM_IN, :, lane0:lane0 + CH]
            acc_ref[c] += _dot(vt, p)

    def finalize(g, par):
        acc_ref = bufs[par][2]
        lam = (jnp.exp(jnp.sum(lq1_ref[...] * lk1_ref[...], axis=-1, keepdims=True))
               - jnp.exp(jnp.sum(lq2_ref[...] * lk2_ref[...], axis=-1, keepdims=True))
               + LAMBDA_INIT)
        a1 = acc_ref[0]
        a2 = acc_ref[1]
        o1 = a1[:DA_V_DIM] / a1[DA_V_DIM:DA_V_DIM + 1]
        o2 = a2[:DA_V_DIM] / a2[DA_V_DIM:DA_V_DIM + 1]
        o = o1 - lam * o2
        ms = jnp.mean(o * o, axis=0, keepdims=True)
        y = o * jax.lax.rsqrt(ms + SUBLN_EPS) * gs_ref[...] * (1.0 - LAMBDA_INIT)
        o_ref[pl.ds(pl.multiple_of(g * TQ, TQ), TQ), :] = y.T.astype(BF16)
        acc_ref[...] = jnp.zeros(acc_ref.shape, F32)

    def step(g, par, scores=True, softmax=True, epilogue=True):
        if epilogue:
            finalize(g - 2, par)
        if scores:
            bufs[par][1][...] = jnp.full((2, 1, TQ), -1e30, F32)
        for r in range(nch):
            if scores:
                scores_chunk(g, r, par)
            if softmax:
                softmax_chunk(g - 1, r, 1 - par)

    nt = ATT_BATCHES * nq
    acca_ref[...] = jnp.zeros(acca_ref.shape, F32)
    accb_ref[...] = jnp.zeros(accb_ref.shape, F32)
    step(0, 0, softmax=False, epilogue=False)
    step(1, 1, epilogue=False)

    def step_pair(j, carry):
        step(2 * j + 2, 0)
        step(2 * j + 3, 1)
        return carry

    jax.lax.fori_loop(0, nt // 2 - 1, step_pair, 0)
    step(nt, 0, scores=False)
    finalize(nt - 1, 1)


def _attention(ch, lq1, lk1, lq2, lk2, gs_col, qt, qaux, kz, kaux, vt):
    nq = SEQ // TQ
    nch = SEQ // CH
    vec = pl.BlockSpec((1, DA_HEAD_DIM), lambda h, b: (0, 0))
    nb = ATT_BATCHES
    return pl.pallas_call(
        _attn_kernel,
        grid=(DA_HEADS, BATCH // nb),
        in_specs=[
            pl.BlockSpec(memory_space=pltpu.SMEM),
            vec, vec, vec, vec,
            pl.BlockSpec((DA_V_DIM, 1), lambda h, b: (0, 0)),
            pl.BlockSpec((nb, 1, nq, 2, 128, TQ), lambda h, b: (b, h, 0, 0, 0, 0)),
            pl.BlockSpec((1, nq, 2, 128, TQ), lambda h, b: (h, 0, 0, 0, 0)),
            pl.BlockSpec((nb * SEQ, 128), lambda h, b: (b, h)),
            pl.BlockSpec((1, SEQ, 128), lambda h, b: (h, 0, 0)),
            pl.BlockSpec((nb, 1, SEQ // TM_IN, V_ROWS, TM_IN), lambda h, b: (b, h, 0, 0, 0)),
        ],
        out_specs=pl.BlockSpec((nb * SEQ, DA_V_DIM), lambda h, b: (b, h)),
        out_shape=jax.ShapeDtypeStruct((BATCH * SEQ, DA_WIDTH), BF16),
        scratch_shapes=[pltpu.VMEM((CORR_ROWS, TQ), F32),
                        pltpu.VMEM((2, SEQ, TQ), F32), pltpu.VMEM((2, SEQ, TQ), F32),
                        pltpu.VMEM((2, 1, TQ), F32), pltpu.VMEM((2, 1, TQ), F32),
                        pltpu.VMEM((2, V_ROWS, TQ), F32), pltpu.VMEM((2, V_ROWS, TQ), F32)],
        compiler_params=pltpu.CompilerParams(
            dimension_semantics=("arbitrary", "arbitrary"), vmem_limit_bytes=VMEM_LIMIT),
        name="diff_attention",
    )(ch, lq1, lk1, lq2, lk2, gs_col, qt, qaux, kz, kaux, vt)


def _seq_dft_kernel(x_ref, w1f_ref, m2f_ref, wup_ref, wdn_ref, wout_ref,
                    o_ref, wup_o_ref, wdn_o_ref, wout_o_ref,
                    w1_ref, m2_ref, xs_ref, a_ref, y_ref):
    nl = DFT_COLS // 128

    @pl.when(jnp.logical_and(pl.program_id(0) == 0, pl.program_id(1) == 0))
    def _prepare_matrices():
        w1_ref[...] = w1f_ref[...].astype(BF16)
        m2_ref[...] = m2f_ref[...].astype(BF16)

    wup_o_ref[...] = wup_ref[...].astype(BF16)
    wdn_o_ref[...] = wdn_ref[...].astype(BF16)
    wout_o_ref[...] = wout_ref[...].astype(BF16)

    def slab(v, l):
        return v[:, l * 128:(l + 1) * 128]

    for n1 in range(64):
        for p in range(2):
            v = x_ref[0, p, n1 * 64:(n1 + 1) * 64, :].astype(F32)
            for l in range(nl):
                xs_ref[p * nl + l, n1 * P1:n1 * P1 + 64, :] = slab(v, l)

    def stage1(n2, carry):
        parts = [jnp.concatenate([xs_ref[p * nl + l, pl.ds(n2, 64, stride=P1), :]
                                  for l in range(nl)], axis=1) for p in range(2)]
        d = jnp.concatenate(parts, axis=0).astype(BF16)
        a = _dot(w1_ref[...], d)
        row0 = pl.multiple_of(n2 * P2, 8)
        for l in range(nl):
            a_ref[l, pl.ds(row0, 128), :] = slab(a, l)
        return carry

    jax.lax.fori_loop(0, 64, stage1, 0, unroll=32)

    def stage2(k1, carry):
        parts = [jnp.concatenate([a_ref[l, pl.ds(k1 + 64 * p, 64, stride=P2), :]
                                  for l in range(nl)], axis=1) for p in range(2)]
        d = jnp.concatenate(parts, axis=0).astype(BF16)
        y = _dot(m2_ref[k1], d)
        row0 = pl.multiple_of(k1 * P1, 8)
        for l in range(nl):
            y_ref[l, pl.ds(row0, 64), :] = slab(y, l)
        return carry

    jax.lax.fori_loop(0, 64, stage2, 0, unroll=32)

    def stage3(k2, carry):
        row0 = pl.multiple_of(k2 * 64, 64)
        for l in range(nl):
            o_ref[pl.ds(row0, 64), l * 128:(l + 1) * 128] = (
                y_ref[l, pl.ds(k2, 64, stride=P1), :].astype(BF16))
        return carry

    jax.lax.fori_loop(0, 64, stage3, 0, unroll=32)


def _seq_dft(xcs, w1, m2, w_up, w_down, w_out):
    nh = FT_WIDTH // DFT_COLS
    nl = DFT_COLS // 128
    steps = BATCH * nh
    slab = lambda b, j: (b * nh + j, 0)

    def row_slab(w):
        return pl.BlockSpec((w.shape[0] // steps, w.shape[1]), slab)

    weights = (w_up, w_down, w_out)
    return pl.pallas_call(
        _seq_dft_kernel,
        grid=(BATCH, nh),
        in_specs=[
            pl.BlockSpec((1, 2, SEQ, DFT_COLS), lambda b, j: (b, 0, 0, j)),
            pl.BlockSpec((128, 128), lambda b, j: (0, 0)),
            pl.BlockSpec((64, 64, 128), lambda b, j: (0, 0, 0), pipeline_mode=pl.Buffered(1)),
        ] + [row_slab(w) for w in weights],
        out_specs=[pl.BlockSpec((SEQ, DFT_COLS), lambda b, j: (b, j))]
        + [row_slab(w) for w in weights],
        out_shape=[jax.ShapeDtypeStruct((BATCH * SEQ, FT_WIDTH), BF16)]
        + [jax.ShapeDtypeStruct(w.shape, BF16) for w in weights],
        scratch_shapes=[pltpu.VMEM((128, 128), BF16), pltpu.VMEM((64, 64, 128), BF16),
                        pltpu.VMEM((2 * nl, 64 * P1, 128), F32),
                        pltpu.VMEM((nl, 64 * P2, 128), F32),
                        pltpu.VMEM((nl, 64 * P1, 128), F32)],
        compiler_params=pltpu.CompilerParams(
            dimension_semantics=("arbitrary", "arbitrary"), vmem_limit_bytes=VMEM_LIMIT),
        name="seq_dft",
    )(xcs, w1, m2, w_up, w_down, w_out)


def _rms(x, g, eps):
    ms = jnp.mean(x * x, axis=-1, keepdims=True)
    return x * jax.lax.rsqrt(ms + eps) * g


def _ffn_kernel(xc_ref, xp_ref, xn_ref, dac_ref, dap_ref, dan_ref, ftc_ref, ftp_ref, ftn_ref,
                wo_ref, gf_ref, wup_ref, wcv_ref, bcv_ref, wdn_ref, gl_ref, o_ref, act_ref):
    i = pl.program_id(0)
    tiles_per_seq = SEQ // TM_FFN
    pos = i % tiles_per_seq
    xe = jnp.concatenate([xp_ref[...], xc_ref[...], xn_ref[...]], axis=0)
    da = jnp.concatenate([dap_ref[...], dac_ref[...], dan_ref[...]], axis=0)
    ft = jnp.concatenate([ftp_ref[...], ftc_ref[...], ftn_ref[...]], axis=0)
    x1 = xe + _dot(da, wo_ref[0:DA_WIDTH, :]) + _dot(ft, wo_ref[DA_WIDTH:, :])
    h2 = _rms(x1, gf_ref[...], EPS)
    r = jax.lax.broadcasted_iota(jnp.int32, (TM_FFN + 2 * HALO, 1), 0)
    valid = jnp.logical_and(jnp.logical_or(r >= HALO, pos > 0),
                            jnp.logical_or(r < TM_FFN + HALO, pos < tiles_per_seq - 1))
    h2 = jnp.where(valid, h2, 0.0).astype(BF16)
    n_ext = TM_FFN + 2 * HALO

    def conv(up, col0):
        w = wcv_ref[:, col0:col0 + FF_CHUNK]
        prev = pltpu.roll(up, 1, 0)[HALO:HALO + TM_FFN]
        nxt = pltpu.roll(up, n_ext - 1, 0)[HALO:HALO + TM_FFN]
        cur = up[HALO:HALO + TM_FFN]
        return (w[0:1] * prev + w[1:2] * cur + w[2:3] * nxt
                + bcv_ref[:, col0:col0 + FF_CHUNK])

    for j in range(D_FF // FF_CHUNK):
        c0 = j * FF_CHUNK
        gate = conv(_dot(h2, wup_ref[:, c0:c0 + FF_CHUNK]), c0)
        val = conv(_dot(h2, wup_ref[:, D_FF + c0:D_FF + c0 + FF_CHUNK]), D_FF + c0)
        act = gate / (1.0 + jnp.exp(-gate)) * val
        act_ref[:, c0:c0 + FF_CHUNK] = act.astype(BF16)

    y = x1[HALO:HALO + TM_FFN] + _dot(act_ref[...], wdn_ref[...])
    o_ref[...] = _rms(y, gl_ref[...], EPS)


def _ffn(x2, o_da, o_ft, w_out, g_ffn, w_up, w_conv, b_conv, w_down, g_final):
    n = BATCH * SEQ // TM_FFN
    hb = TM_FFN // HALO
    last = BATCH * SEQ // HALO - 1
    cur = lambda i: (i, 0)
    prv = lambda i: (jnp.maximum(i * hb - 1, 0), 0)
    nxt = lambda i: (jnp.minimum((i + 1) * hb, last), 0)
    const = lambda i: (0, 0)

    def trio(width):
        return [pl.BlockSpec((TM_FFN, width), cur), pl.BlockSpec((HALO, width), prv),
                pl.BlockSpec((HALO, width), nxt)]

    def resident(shape):
        return pl.BlockSpec(shape, const, pipeline_mode=pl.Buffered(1))

    return pl.pallas_call(
        _ffn_kernel,
        grid=(n,),
        in_specs=trio(D_MODEL) + trio(DA_WIDTH) + trio(FT_WIDTH) + [
            resident((D_MODEL, D_MODEL)),
            resident((1, D_MODEL)),
            resident((D_MODEL, 2 * D_FF)),
            resident((3, 2 * D_FF)),
            resident((1, 2 * D_FF)),
            resident((D_FF, D_MODEL)),
            resident((1, D_MODEL)),
        ],
        out_specs=pl.BlockSpec((TM_FFN, D_MODEL), cur),
        out_shape=jax.ShapeDtypeStruct((BATCH * SEQ, D_MODEL), F32),
        scratch_shapes=[pltpu.VMEM((TM_FFN, D_FF), BF16)],
        compiler_params=pltpu.CompilerParams(
            dimension_semantics=("arbitrary",), vmem_limit_bytes=VMEM_LIMIT),
        name="out_proj_ffn",
    )(x2, x2, x2, o_da, o_da, o_da, o_ft, o_ft, o_ft,
      w_out, g_ffn, w_up, w_conv, b_conv, w_down, g_final)


@functools.lru_cache(maxsize=None)
def _dft_tables():
    i64 = np.arange(64, dtype=np.int64)
    a1 = 2.0 * np.pi * ((i64[:, None] * i64[None, :]) % 64) / 64.0
    c1, s1 = np.cos(a1), np.sin(a1)
    w1 = np.block([[c1, -s1], [-s1, -c1]])
    kk = i64[:, None, None] + 64 * i64[None, :, None]
    a2 = 2.0 * np.pi * ((kk * i64[None, None, :]) % SEQ) / SEQ
    norm = 1.0 / math.sqrt(SEQ)
    m2 = np.concatenate([np.cos(a2) * norm, np.sin(a2) * norm], axis=2)
    c = np.arange(FT_GROUP_DIM, dtype=np.int64)
    ac = 2.0 * np.pi * ((c[:, None] * c[None, :]) % FT_GROUP_DIM) / FT_GROUP_DIM
    cn = 1.0 / math.sqrt(FT_GROUP_DIM)
    cs = np.concatenate([np.cos(ac) * cn, np.sin(ac) * cn], axis=1)
    f = lambda a: np.asarray(a, dtype=np.float32)
    return f(w1), f(m2), f(cs)


def _slope_log2e(h):
    return np.float32(LOG2E * 2.0 ** (-8.0 * (h + 1) / DA_HEADS))


def _bf16_pieces(x, n):
    pieces, rem = [], np.asarray(x, dtype=np.float64)
    for _ in range(n):
        u = rem.astype(np.float32).view(np.uint32)
        u = (u + np.uint32(0x7FFF) + ((u >> np.uint32(16)) & np.uint32(1))) & np.uint32(0xFFFF0000)
        p = u.view(np.float32).astype(np.float64)
        pieces.append(p)
        rem = rem - p
    return pieces


@functools.lru_cache(maxsize=None)
def _alibi_tables():
    pos = np.arange(SEQ, dtype=np.float64)
    kaux = np.zeros((DA_HEADS, SEQ, 128), np.float64)
    qaux = np.zeros((DA_HEADS, 2, 128, SEQ), np.float64)
    for h in range(DA_HEADS):
        pieces = _bf16_pieces(np.float64(_slope_log2e(h)) * pos, N_SPLIT)
        for t, p in enumerate(pieces):
            kaux[h, :, t] = p
            qaux[h, 0, N_SPLIT + t, :] = -p
        kaux[h, :, N_SPLIT:2 * N_SPLIT] = 1.0
        qaux[h, 0, 0:N_SPLIT, :] = 1.0
        qaux[h, 1] = -qaux[h, 0]
    qaux = qaux.reshape(DA_HEADS, 2, 128, SEQ // TQ, TQ).transpose(0, 3, 1, 2, 4)
    return np.asarray(kaux, dtype=BF16), np.ascontiguousarray(qaux).astype(BF16)


def kernel(x, g_mix, w_in, lambda_q1, lambda_k1, lambda_q2, lambda_k2, g_subln, w_ft, w_out,
           g_ffn, w_up, w_conv, b_conv, w_down, g_final):
    x2 = x.reshape(BATCH * SEQ, D_MODEL)
    w1, m2, cs = _dft_tables()
    qt, kz, vt, xcs = _in_proj(x2, g_mix[0][None, :], w_in[0], jnp.asarray(cs), w_ft[0])

    ch = jnp.asarray([_slope_log2e(h) for h in range(DA_HEADS)], F32)
    kaux, qaux = _alibi_tables()
    o_da = _attention(ch, lambda_q1, lambda_k1, lambda_q2, lambda_k2, g_subln[0][:, None],
                      qt, jnp.asarray(qaux), kz, jnp.asarray(kaux), vt)

    o_ft, w_up_bf, w_down_bf, w_out_bf = _seq_dft(xcs, jnp.asarray(w1), jnp.asarray(m2),
                                                  w_up[0], w_down[0], w_out[0])

    y = _ffn(x2, o_da, o_ft, w_out_bf, g_ffn[0][None, :], w_up_bf,
             w_conv[0], b_conv[0][None, :], w_down_bf, g_final[None, :])
    return y.reshape(BATCH, SEQ, D_MODEL)
```
